```python
import math
import jax, jax.numpy as jnp
from jax import lax
import numpy as np

D_MODEL = 1024
BATCH = 16
SEQ = 256
DEPTH = 2
DEC_BATCH = 8
DEC_SEQ = 4096
PAST_LEN = 512

GRID_W = 64
DIFF_HEADS = 4
DIFF_DH = 64
RET_HEADS = 4
RET_DK = 64
RET_DV = 128
RET_CHUNK = 128
MLA_HEADS = 4
MLA_Q_LORA = 384
MLA_KV_LORA = 256
MLA_NOPE = 128
MLA_ROPE = 64
MLA_V = 128
N_BRANCH = 3
BRANCH_W = 512
D_FF = 4 * D_MODEL
Q_BLOCK = 128
ROT_DIM = 64
ROPE_BASE = 10000.0
EPS = 1e-6
SPLIT_SIZES = (
    DIFF_HEADS * 2 * DIFF_DH,
    DIFF_HEADS * 2 * DIFF_DH,
    DIFF_HEADS * 2 * DIFF_DH,
    RET_HEADS * RET_DK,
    RET_HEADS * RET_DK,
    RET_HEADS * RET_DV,
    RET_HEADS * RET_DV,
    MLA_Q_LORA,
    MLA_KV_LORA,
    MLA_ROPE,
    N_BRANCH * D_MODEL,
)
D_IN = sum(SPLIT_SIZES)

kernel_name = "hybrid_diffretmla_diffusion_step"

F32 = jnp.float32


def rmsnorm(x, g=None):
    x32 = x.astype(F32)
    y = x32 * lax.rsqrt(jnp.mean(jnp.square(x32), axis=-1, keepdims=True) + EPS)
    if g is not None:
        y = y * g.astype(F32)
    return y.astype(x.dtype)


def axial_rope_tables(n):
    n_rows = n // GRID_W
    row = jnp.repeat(jnp.arange(n_rows, dtype=F32), GRID_W)
    col = jnp.tile(jnp.arange(GRID_W, dtype=F32), n_rows)
    axis_dim = ROT_DIM // 2
    inv = ROPE_BASE ** (-jnp.arange(0, axis_dim, 2, dtype=F32) / axis_dim)
    ang_r = row[:, None] * inv[None, :]
    ang_c = col[:, None] * inv[None, :]
    return (jnp.cos(ang_r), jnp.sin(ang_r), jnp.cos(ang_c), jnp.sin(ang_c))


def _rotate(x, cos, sin):
    x1, x2 = jnp.split(x, 2, axis=-1)
    cos = cos[:, None, :].astype(x.dtype)
    sin = sin[:, None, :].astype(x.dtype)
    return jnp.concatenate([x1 * cos - x2 * sin, x2 * cos + x1 * sin], axis=-1)


def apply_axial_rope(x, tabs):
    cr, sr, cc, sc = tabs
    xr, xc = jnp.split(x, 2, axis=-1)
    return jnp.concatenate([_rotate(xr, cr, sr), _rotate(xc, cc, sc)], axis=-1)


def over_query_blocks(fn, q):
    B, N = q.shape[:2]
    nb = N // Q_BLOCK
    qb = jnp.moveaxis(q.reshape((B, nb, Q_BLOCK) + q.shape[2:]), 1, 0)
    ob = lax.map(fn, qb)
    return jnp.moveaxis(ob, 0, 1).reshape((B, N) + ob.shape[3:])


def diff_attention(q, k, v, lam):
    scale = DIFF_DH ** -0.5

    def blk(qb):
        s = jnp.einsum('bqgd,bkgd->bgqk', qb, k).astype(F32) * scale
        p = jax.nn.softmax(s, axis=-1)
        p = p.reshape(p.shape[0], DIFF_HEADS, 2, p.shape[2], p.shape[3])
        a = (p[:, :, 0] - lam * p[:, :, 1]).astype(v.dtype)
        return jnp.einsum('bhqk,bkhe->bqhe', a, v)

    return over_query_blocks(blk, q)


def softmax_attention(q, k, v, scale):
    def blk(qb):
        s = jnp.einsum('bqhd,bkhd->bhqk', qb, k).astype(F32) * scale
        p = jax.nn.softmax(s, axis=-1).astype(v.dtype)
        return jnp.einsum('bhqk,bkhe->bqhe', p, v)

    return over_query_blocks(blk, q)


def retention_scan(q, k, v, log_gamma, r0, strict):
    out_dtype = v.dtype
    B, N, H, _ = q.shape
    dv = v.shape[-1]
    C = RET_CHUNK
    nc = N // C

    def chunks(t):
        return jnp.moveaxis(t.astype(F32).reshape(B, nc, C, H, t.shape[-1]), 1, 0)

    idx = jnp.arange(C, dtype=F32)
    rel = idx[:, None] - idx[None, :]
    mask = (rel > 0) if strict else (rel >= 0)
    decay_in = jnp.where(mask[None], jnp.exp(jnp.where(mask, rel, 0.0)[None] * log_gamma[:, None, None]), 0.0)
    q_dec = jnp.exp((idx + 1.0)[:, None] * log_gamma[None, :])
    k_dec = jnp.exp((C - 1.0 - idx)[:, None] * log_gamma[None, :])
    chunk_dec = jnp.exp(C * log_gamma)[None, :, None, None]

    def step(R, xs):
        qc, kc, vc = xs
        s = jnp.einsum('bihd,bjhd->bhij', qc, kc) * decay_in[None]
        o = (jnp.einsum('bhij,bjhe->bihe', s, vc)
             + jnp.einsum('bihd,bhde->bihe', qc * q_dec[None, :, :, None], R))
        R = chunk_dec * R + jnp.einsum('bjhd,bjhe->bhde', kc * k_dec[None, :, :, None], vc)
        return R, o

    R, o = lax.scan(step, r0.astype(F32), (chunks(q), chunks(k), chunks(v)))
    o = jnp.moveaxis(o, 0, 1).reshape(B, N, H, dv).astype(out_dtype)
    return o, R


def mla_keys(ckv, kr, w_kvb, kn):
    B, K = ckv.shape[:2]
    kv = (ckv @ w_kvb).reshape(B, K, MLA_HEADS, MLA_NOPE + MLA_V)
    k_nope, v = jnp.split(kv, [MLA_NOPE], axis=-1)
    k = jnp.concatenate([k_nope, jnp.broadcast_to(kr[:, :, None, :], (B, K, MLA_HEADS, MLA_ROPE))], axis=-1)
    return rmsnorm(k, kn), v


def token_mixers(h, lw, lam_init, rope, ctx):
    B, N, _ = h.shape
    latent = ctx is not None
    offs = np.cumsum(SPLIT_SIZES)[:-1].tolist()
    z = h @ lw['w_in']
    d_q, d_k, d_v, r_q, r_k, r_v, r_g, m_qa, m_kva, m_kr, g = jnp.split(z, offs, axis=-1)

    dq = rmsnorm(d_q.reshape(B, N, 2 * DIFF_HEADS, DIFF_DH), lw['diff_qn'])
    dk = rmsnorm(d_k.reshape(B, N, 2 * DIFF_HEADS, DIFF_DH), lw['diff_kn'])
    dv = d_v.reshape(B, N, DIFF_HEADS, 2 * DIFF_DH)
    if latent:
        dq = apply_axial_rope(dq, rope)
        dk_all = jnp.concatenate([apply_axial_rope(dk, rope), ctx[0]], axis=1)
        dv_all = jnp.concatenate([dv, ctx[1]], axis=1)
    else:
        dk_all, dv_all = dk, dv
    lmb = lw['diff_lambda'].astype(F32)
    lam = jnp.exp(jnp.sum(lmb[0] * lmb[1])) - jnp.exp(jnp.sum(lmb[2] * lmb[3])) + lam_init
    oa = diff_attention(dq, dk_all, dv_all, lam)
    oa = (rmsnorm(oa, lw['diff_subln']) * (1.0 - lam_init)).reshape(B, N, BRANCH_W)

    rq = r_q.reshape(B, N, RET_HEADS, RET_DK)
    rk = r_k.reshape(B, N, RET_HEADS, RET_DK) * (RET_DK ** -0.5)
    rv = r_v.reshape(B, N, RET_HEADS, RET_DV)
    if latent:
        rq = apply_axial_rope(rq, rope)
        rk = apply_axial_rope(rk, rope)
        r0f, r0b = ctx[4][:, 0], ctx[4][:, 1]
    else:
        r0f = jnp.zeros((B, RET_HEADS, RET_DK, RET_DV), F32)
        r0b = r0f
    lg = jax.nn.log_sigmoid(lw['ret_decay'].astype(F32))
    of, Rf = retention_scan(rq, rk, rv, lg[0], r0f, strict=False)
    ob, Rb = retention_scan(jnp.flip(rq, 1), jnp.flip(rk, 1), jnp.flip(rv, 1), lg[1], r0b, strict=True)
    ob = jnp.flip(ob, 1)
    orr = rmsnorm(of + ob, lw['ret_gn']) * jax.nn.silu(r_g.reshape(B, N, RET_HEADS, RET_DV))
    orr = orr.reshape(B, N, BRANCH_W)

    mq = (rmsnorm(m_qa, lw['mla_qa_norm']) @ lw['w_mla_qb']).reshape(B, N, MLA_HEADS, MLA_NOPE + MLA_ROPE)
    mq = rmsnorm(mq, lw['mla_qn'])
    ckv = rmsnorm(m_kva, lw['mla_kva_norm'])
    mk, mv = mla_keys(ckv, m_kr, lw['w_mla_kvb'], lw['mla_kn'])
    if latent:
        mq = jnp.concatenate([mq[..., :MLA_NOPE], apply_axial_rope(mq[..., MLA_NOPE:], rope)], axis=-1)
        mk = jnp.concatenate([mk[..., :MLA_NOPE], apply_axial_rope(mk[..., MLA_NOPE:], rope)], axis=-1)
        ck, cv = mla_keys(ctx[2], ctx[3], lw['w_mla_kvb'], lw['mla_kn'])
        mk = jnp.concatenate([mk, ck], axis=1)
        mv = jnp.concatenate([mv, cv], axis=1)
    om = softmax_attention(mq, mk, mv, (MLA_NOPE + MLA_ROPE) ** -0.5).reshape(B, N, BRANCH_W)

    gates = jnp.split(g, N_BRANCH, axis=-1)
    merged = 0.0
    for i, o in enumerate((oa, orr, om)):
        merged = merged + jax.nn.sigmoid(gates[i]) * (o @ lw['w_branch'][i])
    out = merged @ lw['w_out']

    if latent:
        return out, None
    return out, (dk, dv, ckv, m_kr, jnp.stack([Rf, Rb], axis=1))


def block(x, cvec, lw, lam_init, rope, ctx):
    mod = (jax.nn.silu(cvec) @ lw['w_mod'] + lw['b_mod'])[:, None, :]
    sh1, sc1, g1, sh2, sc2, g2 = jnp.split(mod, 6, axis=-1)
    h = rmsnorm(x, lw['norm1']) * (1.0 + sc1) + sh1
    mix, ctx_out = token_mixers(h, lw, lam_init, rope, ctx)
    x = x + g1 * mix
    h = rmsnorm(x, lw['norm2']) * (1.0 + sc2) + sh2
    u = jnp.square(jax.nn.relu(h @ lw['w_up']))
    x = x + g2 * (u @ lw['w_down'])
    return x, ctx_out


def setup_inputs(seed: int = 0) -> dict:
    key = jax.random.key(seed)
    ks = iter(jax.random.split(key, 40))

    def nrm(shape, scale=1.0):
        return jax.random.normal(next(ks), shape, F32) * scale

    def gain(shape):
        return 1.0 + nrm(shape, 0.01)

    a = 5.0 + jnp.arange(RET_HEADS, dtype=F32)
    decay_logit = jnp.log(2.0 ** a - 1.0)
    return {
        "x_prompt": nrm((BATCH, SEQ, D_MODEL)),
        "x_sample": nrm((DEC_BATCH, DEC_SEQ, D_MODEL)),
        "cache_diff_k": nrm((DEC_BATCH, DEPTH, PAST_LEN, 2 * DIFF_HEADS, DIFF_DH)),
        "cache_diff_v": nrm((DEC_BATCH, DEPTH, PAST_LEN, DIFF_HEADS, 2 * DIFF_DH)),
        "cache_mla_ckv": nrm((DEC_BATCH, DEPTH, PAST_LEN, MLA_KV_LORA)),
        "cache_mla_krope": nrm((DEC_BATCH, DEPTH, PAST_LEN, MLA_ROPE)),
        "state_ret": nrm((DEC_BATCH, DEPTH, 2, RET_HEADS, RET_DK, RET_DV), 0.5),
        "c": nrm((DEC_BATCH, D_MODEL)),
        "c_ctx": nrm((D_MODEL,)),
        "w_mod": nrm((DEPTH, D_MODEL, 6 * D_MODEL), 0.5 * D_MODEL ** -0.5),
        "b_mod": nrm((DEPTH, 6 * D_MODEL), 0.01),
        "norm1": gain((DEPTH, D_MODEL)),
        "norm2": gain((DEPTH, D_MODEL)),
        "w_in": nrm((DEPTH, D_MODEL, D_IN), D_MODEL ** -0.5),
        "diff_qn": gain((DEPTH, DIFF_DH)),
        "diff_kn": gain((DEPTH, DIFF_DH)),
        "diff_lambda": nrm((DEPTH, 4, DIFF_DH), 0.1),
        "diff_subln": gain((DEPTH, 2 * DIFF_DH)),
        "ret_decay": decay_logit[None, None, :] + nrm((DEPTH, 2, RET_HEADS), 0.1),
        "ret_gn": gain((DEPTH, RET_DV)),
        "mla_qa_norm": gain((DEPTH, MLA_Q_LORA)),
        "w_mla_qb": nrm((DEPTH, MLA_Q_LORA, MLA_HEADS * (MLA_NOPE + MLA_ROPE)), MLA_Q_LORA ** -0.5),
        "mla_kva_norm": gain((DEPTH, MLA_KV_LORA)),
        "w_mla_kvb": nrm((DEPTH, MLA_KV_LORA, MLA_HEADS * (MLA_NOPE + MLA_V)), MLA_KV_LORA ** -0.5),
        "mla_qn": gain((DEPTH, MLA_NOPE + MLA_ROPE)),
        "mla_kn": gain((DEPTH, MLA_NOPE + MLA_ROPE)),
        "w_branch": nrm((DEPTH, N_BRANCH, BRANCH_W, D_MODEL), BRANCH_W ** -0.5),
        "w_out": nrm((DEPTH, D_MODEL, D_MODEL), D_MODEL ** -0.5),
        "w_up": nrm((DEPTH, D_MODEL, D_FF), D_MODEL ** -0.5),
        "w_down": nrm((DEPTH, D_FF, D_MODEL), D_FF ** -0.5),
    }


def reference(x_prompt, x_sample, cache_diff_k, cache_diff_v, cache_mla_ckv, cache_mla_krope, state_ret,
              c, c_ctx, w_mod, b_mod, norm1, norm2, w_in, diff_qn, diff_kn, diff_lambda, diff_subln,
              ret_decay, ret_gn, mla_qa_norm, w_mla_qb, mla_kva_norm, w_mla_kvb, mla_qn, mla_kn,
              w_branch, w_out, w_up, w_down):
    def layer_weights(l):
        return {
            'w_mod': w_mod[l], 'b_mod': b_mod[l], 'norm1': norm1[l], 'norm2': norm2[l],
            'w_in': w_in[l], 'diff_qn': diff_qn[l], 'diff_kn': diff_kn[l],
            'diff_lambda': diff_lambda[l], 'diff_subln': diff_subln[l],
            'ret_decay': ret_decay[l], 'ret_gn': ret_gn[l],
            'mla_qa_norm': mla_qa_norm[l], 'w_mla_qb': w_mla_qb[l], 'mla_kva_norm': mla_kva_norm[l],
            'w_mla_kvb': w_mla_kvb[l], 'mla_qn': mla_qn[l], 'mla_kn': mla_kn[l],
            'w_branch': w_branch[l], 'w_out': w_out[l], 'w_up': w_up[l], 'w_down': w_down[l],
        }

    def lambda_init(l):
        return 0.8 - 0.6 * math.exp(-0.3 * l)

    xp = x_prompt
    ks_, vs_, ckvs_, krs_, rs_ = [], [], [], [], []
    for l in range(DEPTH):
        xp, ctx_out = block(xp, c_ctx[None, :], layer_weights(l), lambda_init(l), None, None)
        ks_.append(ctx_out[0]); vs_.append(ctx_out[1]); ckvs_.append(ctx_out[2])
        krs_.append(ctx_out[3]); rs_.append(ctx_out[4])
    y_prompt = xp
    new_diff_k = jnp.stack(ks_, axis=1)
    new_diff_v = jnp.stack(vs_, axis=1)
    new_mla_ckv = jnp.stack(ckvs_, axis=1)
    new_mla_krope = jnp.stack(krs_, axis=1)
    new_state_ret = jnp.stack(rs_, axis=1)

    rope = axial_rope_tables(x_sample.shape[1])
    xs = x_sample
    for l in range(DEPTH):
        ctx = (cache_diff_k[:, l], cache_diff_v[:, l], cache_mla_ckv[:, l], cache_mla_krope[:, l], state_ret[:, l])
        xs, _ = block(xs, c, layer_weights(l), lambda_init(l), rope, ctx)
    y_sample = xs

    return (y_prompt, y_sample, new_diff_k, new_diff_v, new_mla_ckv, new_mla_krope, new_state_ret)
```

```python
import functools
import math

import jax
import jax.numpy as jnp
from jax import lax
from jax.experimental import pallas as pl
from jax.experimental.pallas import tpu as pltpu

F32 = jnp.float32
BF16 = jnp.bfloat16

D_MODEL = 1024
GRID_W = 64
DIFF_HEADS = 4
DIFF_DH = 64
RET_HEADS = 4
RET_DK = 64
RET_DV = 128
RET_CHUNK = 128
MLA_HEADS = 4
MLA_Q_LORA = 384
MLA_KV_LORA = 256
MLA_NOPE = 128
MLA_ROPE = 64
MLA_V = 128
MLA_QK = MLA_NOPE + MLA_ROPE
N_BRANCH = 3
BRANCH_W = 512
D_FF = 4 * D_MODEL
ROT_DIM = 64
ROPE_BASE = 10000.0
EPS = 1e-6
LAMBDA_INIT_BASE = (0.8, 0.6, 0.3)

LANES = 128
MLA_PAD = 2 * LANES
HEAD_W = 128
LOG2E = 1.4426950408889634
VMEM_LIMIT = 56 * 1024 * 1024
MOD_ROWS = 16


def _resident(shape):
    nd = len(shape)
    return pl.BlockSpec(shape, lambda *_: (0,) * nd, pipeline_mode=pl.Buffered(1))


def _params(n_grid):
    return pltpu.CompilerParams(dimension_semantics=("arbitrary",) * n_grid,
                                vmem_limit_bytes=VMEM_LIMIT)


def _sigmoid(x):
    return 1.0 / (1.0 + jnp.exp(-x))


def _rms(x, width):
    ss = jnp.sum(x * x, axis=-1, keepdims=True)
    return x * lax.rsqrt(ss * (1.0 / width) + EPS)


def _mod_kernel(c_ref, w_ref, b_ref, o_ref):
    c = c_ref[...]
    a = (c * _sigmoid(c)).astype(BF16)
    o_ref[0] = jnp.dot(a, w_ref[0].astype(BF16), preferred_element_type=F32) + b_ref[0]


def _modulation(cvec, w_mod, b_mod):
    n_layers, d, d6 = w_mod.shape
    tn = 1024
    return pl.pallas_call(
        _mod_kernel,
        grid=(n_layers, d6 // tn),
        in_specs=[
            pl.BlockSpec((MOD_ROWS, d), lambda l, j: (0, 0)),
            pl.BlockSpec((1, d, tn), lambda l, j: (l, 0, j)),
            pl.BlockSpec((1, 1, tn), lambda l, j: (l, 0, j)),
        ],
        out_specs=pl.BlockSpec((1, MOD_ROWS, tn), lambda l, j: (l, 0, j)),
        out_shape=jax.ShapeDtypeStruct((n_layers, MOD_ROWS, d6), F32),
        compiler_params=_params(2),
        name="modulation",
    )(cvec, w_mod, b_mod.reshape(n_layers, 1, d6))


def _inproj_kernel(*refs, latent):
    it = iter(refs)
    x_ref, mod_ref, n1_ref = next(it), next(it), next(it)
    if latent:
        cos_ref, sa_ref, sb_ref = next(it), next(it), next(it)
    w_diff, w_ret, w_mla, w_gate, w_qb, w_kn, w_v = (next(it) for _ in range(7))
    dqn_ref, dkn_ref, qag_ref, kvag_ref, mqn_ref, mkn_ref = (next(it) for _ in range(6))
    (dq_ref, dk_ref, dv_ref, rq_ref, rk_ref, rv_ref, rg_ref,
     mq_ref, mk_ref, mv_ref, gate_ref) = (next(it) for _ in range(11))
    if not latent:
        odk_ref, odv_ref, ockv_ref, okr_ref = (next(it) for _ in range(4))

    x = x_ref[0]
    mod = mod_ref[0, 0]
    h = _rms(x, D_MODEL) * n1_ref[...] * (1.0 + mod[1:2]) + mod[0:1]
    hb = h.astype(BF16)

    if latent:
        cos, sa, sb = cos_ref[...], sa_ref[...], sb_ref[...]

        def rope(y):
            return (y * cos + pltpu.roll(y, LANES - 16, 1) * sa + pltpu.roll(y, 16, 1) * sb)
    else:
        def rope(y):
            return y

    def group(z, g):
        return z[:, g * LANES:(g + 1) * LANES]

    r_i = lax.broadcasted_iota(jnp.int32, (LANES, LANES), 0) // DIFF_DH
    c_i = lax.broadcasted_iota(jnp.int32, (LANES, LANES), 1) // DIFF_DH
    seg = jnp.where(r_i == c_i, 1.0, 0.0).astype(BF16)

    def qk_norm(y, gain):
        ss = jnp.dot((y * y).astype(BF16), seg, preferred_element_type=F32)
        return y * lax.rsqrt(ss * (1.0 / DIFF_DH) + EPS) * gain

    zd = jnp.dot(hb, w_diff[...], preferred_element_type=F32)
    n_g = BRANCH_W // LANES
    q_gain = dqn_ref[...] * (DIFF_DH ** -0.5 * LOG2E)
    for g in range(n_g):
        sl = slice(g * LANES, (g + 1) * LANES)
        dq_ref[0, :, sl] = rope(qk_norm(group(zd, g), q_gain)).astype(BF16)
        kn = qk_norm(group(zd, n_g + g), dkn_ref[...])
        dk_ref[0, :, sl] = rope(kn).astype(BF16)
        v = group(zd, 2 * n_g + g)
        dv_ref[0, :, sl] = v.astype(BF16)
        if not latent:
            odk_ref[0, :, sl] = kn
            odv_ref[0, :, sl] = v

    zr = jnp.dot(hb, w_ret[...], preferred_element_type=F32)
    for g in range(n_g):
        sl = slice(g * LANES, (g + 1) * LANES)
        rq_ref[0, :, sl] = rope(group(zr, g)).astype(BF16)
        rk_ref[0, :, sl] = (rope(group(zr, n_g + g)) * RET_DK ** -0.5).astype(BF16)
        rv_ref[0, :, sl] = group(zr, 2 * n_g + g).astype(BF16)
        rg_ref[0, :, sl] = group(zr, 3 * n_g + g).astype(BF16)

    zm = jnp.dot(hb, w_mla[...], preferred_element_type=F32)
    qa = _rms(zm[:, :MLA_Q_LORA], MLA_Q_LORA) * qag_ref[...]
    mqz = jnp.dot(qa.astype(BF16), w_qb[...], preferred_element_type=F32)
    ckv = _rms(zm[:, MLA_Q_LORA:MLA_Q_LORA + MLA_KV_LORA], MLA_KV_LORA) * kvag_ref[...]
    ckvb = ckv.astype(BF16)
    k_nope = jnp.dot(ckvb, w_kn[...], preferred_element_type=F32)
    mv_ref[0] = jnp.dot(ckvb, w_v[...], preferred_element_type=F32).astype(BF16)
    kr = zm[:, MLA_Q_LORA + MLA_KV_LORA:]
    ss_kr = jnp.sum(kr * kr, axis=-1, keepdims=True)
    if not latent:
        ockv_ref[0] = ckv
        okr_ref[0] = kr[:, :MLA_ROPE]
    q_gain = mqn_ref[...] * (MLA_QK ** -0.5 * LOG2E)
    k_gain = mkn_ref[...]
    for hd in range(MLA_HEADS):
        lo = slice(hd * MLA_PAD, hd * MLA_PAD + LANES)
        hi = slice(hd * MLA_PAD + LANES, (hd + 1) * MLA_PAD)
        qh = _rms(mqz[:, hd * MLA_PAD:(hd + 1) * MLA_PAD], MLA_QK) * q_gain
        mq_ref[0, :, lo] = qh[:, :LANES].astype(BF16)
        mq_ref[0, :, hi] = rope(qh[:, LANES:]).astype(BF16)
        kno = k_nope[:, hd * LANES:(hd + 1) * LANES]
        ss = jnp.sum(kno * kno, axis=-1, keepdims=True) + ss_kr
        rstd = lax.rsqrt(ss * (1.0 / MLA_QK) + EPS)
        mk_ref[0, :, lo] = (kno * rstd * k_gain[:, :LANES]).astype(BF16)
        mk_ref[0, :, hi] = rope(kr * rstd * k_gain[:, LANES:]).astype(BF16)

    gate_ref[0] = jnp.dot(hb, w_gate[...], preferred_element_type=F32).astype(BF16)


def _inproj(x, mod, layer, lw, rope_tabs, tm):
    b_sz, n, d = x.shape
    latent = rope_tabs is not None
    tm = min(tm, n)
    grid = (b_sz, n // tm)

    def tok(width):
        return pl.BlockSpec((1, tm, width), lambda b, i: (b, i, 0))

    if latent:
        mod_spec = pl.BlockSpec((1, 1, 6, d), lambda b, i: (layer, 1 + b, 0, 0))
    else:
        mod_spec = pl.BlockSpec((1, 1, 6, d), lambda b, i: (layer, 0, 0, 0))
    in_specs = [tok(d), mod_spec, _resident((1, d))]
    args = [x, mod, lw["norm1"]]
    if latent:
        in_specs += [pl.BlockSpec((tm, LANES), lambda b, i: (i, 0))] * 3
        args += list(rope_tabs)
    weights = [lw["w_diff"], lw["w_ret"], lw["w_mla"], lw["w_gate"], lw["w_qb"], lw["w_kn"], lw["w_v"]]
    gains = [lw["dqn"], lw["dkn"], lw["qa_g"], lw["kva_g"], lw["mqn"], lw["mkn"]]
    for a in weights + gains:
        in_specs.append(_resident(a.shape))
        args.append(a)

    widths = [BRANCH_W] * 7 + [MLA_HEADS * MLA_PAD] * 2 + [MLA_HEADS * MLA_V, N_BRANCH * d]
    out_specs = [tok(w) for w in widths]
    out_shape = [jax.ShapeDtypeStruct((b_sz, n, w), BF16) for w in widths]
    if not latent:
        for w in (BRANCH_W, BRANCH_W, MLA_KV_LORA, MLA_ROPE):
            out_specs.append(tok(w))
            out_shape.append(jax.ShapeDtypeStruct((b_sz, n, w), F32))
    return pl.pallas_call(
        functools.partial(_inproj_kernel, latent=latent),
        grid=grid, in_specs=in_specs, out_specs=out_specs, out_shape=out_shape,
        compiler_params=_params(2),
        name="inproj_latent" if latent else "inproj_context",
    )(*args)


def _attn_kernel(*refs, diff, has_cache, n_new, n_past, tq, ck, lam_init):
    it = iter(refs)
    q_ref, kn_ref, vn_ref = next(it), next(it), next(it)
    if has_cache:
        kc_ref, vc_ref = next(it), next(it)
    if diff:
        lmb_ref, sub_ref = next(it), next(it)
    o_ref, s_scr = next(it), next(it)

    q = q_ref[0]
    if diff:
        lane = lax.broadcasted_iota(jnp.int32, q.shape, 1)
        qf = q.astype(F32)
        q = jnp.concatenate([jnp.where(lane < DIFF_DH, qf, 0.0), jnp.where(lane >= DIFF_DH, qf, 0.0)],
                            axis=0).astype(BF16)
    rows = q.shape[0]

    chunks = []
    for j in range(n_new // ck):
        chunks.append((lambda j=j: kn_ref[0, j * ck:(j + 1) * ck, :], lambda j=j: vn_ref[0, j * ck:(j + 1) * ck, :]))
    if has_cache:
        for j in range(n_past // ck):
            chunks.append((lambda j=j: kc_ref[0, 0, j * ck:(j + 1) * ck, :],
                           lambda j=j: vc_ref[0, 0, j * ck:(j + 1) * ck, :]))

    m_acc = jnp.full((rows, LANES), -jnp.inf, F32)
    for j, (k_of, _) in enumerate(chunks):
        s = lax.dot_general(q, k_of(), (((1,), (1,)), ((), ())), preferred_element_type=F32)
        s_scr[:, j * ck:(j + 1) * ck] = s
        for t in range(ck // LANES):
            m_acc = jnp.maximum(m_acc, s[:, t * LANES:(t + 1) * LANES])
    m = jnp.max(m_acc, axis=-1, keepdims=True)

    l_acc = jnp.zeros((rows, LANES), F32)
    acc = jnp.zeros((rows, HEAD_W), F32)
    for j, (_, v_of) in enumerate(chunks):
        e = jnp.exp2(s_scr[:, j * ck:(j + 1) * ck] - m)
        for t in range(ck // LANES):
            l_acc = l_acc + e[:, t * LANES:(t + 1) * LANES]
        acc = acc + jnp.dot(e.astype(BF16), v_of(), preferred_element_type=F32)
    o = acc / jnp.sum(l_acc, axis=-1, keepdims=True)

    if diff:
        lmb = lmb_ref[...]
        lam = (jnp.exp(jnp.sum(lmb[0:1] * lmb[1:2], axis=-1, keepdims=True))
               - jnp.exp(jnp.sum(lmb[2:3] * lmb[3:4], axis=-1, keepdims=True)) + lam_init)
        o = o[:tq] - lam * o[tq:]
        o = _rms(o, HEAD_W) * sub_ref[...] * (1.0 - lam_init)
    o_ref[0] = o.astype(BF16)


def _attention(q, k_new, v_new, cache, layer, *, diff, lam_init=0.0, lmb=None, subln=None, tq=256, ck=512):
    b_sz, n, _ = q.shape
    dq = LANES if diff else MLA_PAD
    n_heads = DIFF_HEADS if diff else MLA_HEADS
    tq = min(tq, n)
    has_cache = cache is not None
    n_past = cache[0].shape[2] if has_cache else 0
    ck = min(ck, n, n_past) if has_cache else min(ck, n)
    grid = (b_sz, n_heads, n // tq)

    in_specs = [
        pl.BlockSpec((1, tq, dq), lambda b, h, i: (b, i, h)),
        pl.BlockSpec((1, n, dq), lambda b, h, i: (b, 0, h)),
        pl.BlockSpec((1, n, HEAD_W), lambda b, h, i: (b, 0, h)),
    ]
    args = [q, k_new, v_new]
    if has_cache:
        in_specs += [
            pl.BlockSpec((1, 1, n_past, dq), lambda b, h, i: (b, layer, 0, h)),
            pl.BlockSpec((1, 1, n_past, HEAD_W), lambda b, h, i: (b, layer, 0, h)),
        ]
        args += list(cache)
    if diff:
        in_specs += [_resident(lmb.shape), _resident(subln.shape)]
        args += [lmb, subln]
    rows = 2 * tq if diff else tq
    return pl.pallas_call(
        functools.partial(_attn_kernel, diff=diff, has_cache=has_cache, n_new=n, n_past=n_past,
                          tq=tq, ck=ck, lam_init=lam_init),
        grid=grid, in_specs=in_specs,
        out_specs=pl.BlockSpec((1, tq, HEAD_W), lambda b, h, i: (b, i, h)),
        out_shape=jax.ShapeDtypeStruct((b_sz, n, n_heads * HEAD_W), BF16),
        scratch_shapes=[pltpu.VMEM((rows, n + n_past), F32)],
        compiler_params=_params(3),
        name=("diff_attn" if diff else "mla_attn") + ("_latent" if has_cache else "_context"),
    )(*args)


def _log_sigmoid(x):
    return jnp.minimum(x, 0.0) - jnp.log(1.0 + jnp.exp(-jnp.abs(x)))


def _ret_kernel(*refs, latent, n_chunks):
    it = iter(refs)
    dec_ref, q_ref, k_ref, v_ref, g_ref, gn_ref = (next(it) for _ in range(6))
    if latent:
        r0_ref = next(it)
    o_ref = next(it)
    if not latent:
        st_ref = next(it)
    rcat_scr, u_scr = next(it), next(it)

    c_len = RET_CHUNK
    hd = pl.program_id(1)
    lg_f = _log_sigmoid(jnp.full((1, 1), dec_ref[0, hd], F32))
    lg_b = _log_sigmoid(jnp.full((1, 1), dec_ref[1, hd], F32))
    row = lax.broadcasted_iota(jnp.int32, (c_len, c_len), 0).astype(F32)
    col = lax.broadcasted_iota(jnp.int32, (c_len, c_len), 1).astype(F32)
    lane_fwd = lax.broadcasted_iota(jnp.int32, (c_len, 2 * RET_DK), 1) < RET_DK
    d_in = 0.5 * jnp.where(row >= col, jnp.exp((row - col) * lg_f), jnp.exp((col - row) * lg_b))
    q_dec = jnp.where(lane_fwd, jnp.exp((row + 1.0) * lg_f), jnp.exp((c_len - row) * lg_b))
    k_dec = jnp.where(lane_fwd, jnp.exp((c_len - 1.0 - row) * lg_f), jnp.exp(row * lg_b))
    dec_f = jnp.exp(c_len * lg_f)
    dec_b = jnp.exp(c_len * lg_b)

    def chunk(c):
        return pl.ds(pl.multiple_of(c * c_len, c_len), c_len)

    def inc_body(c, carry):
        kd = (k_ref[0, chunk(c), :].astype(F32) * k_dec).astype(BF16)
        u_scr[c] = lax.dot_general(kd, v_ref[0, chunk(c), :], (((0,), (0,)), ((), ())),
                                   preferred_element_type=F32)
        return carry

    lax.fori_loop(0, n_chunks, inc_body, 0)

    if latent:
        r_f0 = r0_ref[0, 0, 0, 0]
        r_b0 = r0_ref[0, 0, 1, 0]
    else:
        r_f0 = jnp.zeros((RET_DK, RET_DV), F32)
        r_b0 = r_f0

    def fwd_body(c, r):
        rcat_scr[c, 0:RET_DK, :] = r.astype(BF16)
        return dec_f * r + u_scr[c, 0:RET_DK, :]

    def bwd_body(t, r):
        c = n_chunks - 1 - t
        rcat_scr[c, RET_DK:2 * RET_DK, :] = r.astype(BF16)
        return dec_b * r + u_scr[c, RET_DK:2 * RET_DK, :]

    r_f = lax.fori_loop(0, n_chunks, fwd_body, r_f0)
    r_b = lax.fori_loop(0, n_chunks, bwd_body, r_b0)
    if not latent:
        st_ref[0, 0, 0] = r_f
        st_ref[0, 1, 0] = r_b

    gn = gn_ref[...]

    def out_body(c, carry):
        qc = q_ref[0, chunk(c), :]
        kc = k_ref[0, chunk(c), :]
        vc = v_ref[0, chunk(c), :]
        a = lax.dot_general(qc, kc, (((1,), (1,)), ((), ())), preferred_element_type=F32) * d_in
        qd = (qc.astype(F32) * q_dec).astype(BF16)
        o = (jnp.dot(a.astype(BF16), vc, preferred_element_type=F32)
             + jnp.dot(qd, rcat_scr[c], preferred_element_type=F32))
        gate = g_ref[0, chunk(c), :].astype(F32)
        y = _rms(o, RET_DV) * gn * (gate * _sigmoid(gate))
        o_ref[0, chunk(c), :] = y.astype(BF16)
        return carry

    lax.fori_loop(0, n_chunks, out_body, 0)


def _retention(rq, rk, rv, rg, ret_decay, ret_gn, state, layer):
    b_sz, n, _ = rq.shape
    latent = state is not None
    n_chunks = n // RET_CHUNK
    tokw = pl.BlockSpec((1, n, HEAD_W), lambda b, h: (b, 0, h))
    in_specs = [pl.BlockSpec(memory_space=pltpu.SMEM), tokw, tokw, tokw, tokw, _resident(ret_gn.shape)]
    args = [ret_decay, rq, rk, rv, rg, ret_gn]
    if latent:
        in_specs.append(pl.BlockSpec((1, 1, 2, 1, RET_DK, RET_DV), lambda b, h: (b, layer, 0, h, 0, 0)))
        args.append(state)
    out_specs = [tokw]
    out_shape = [jax.ShapeDtypeStruct((b_sz, n, RET_HEADS * HEAD_W), BF16)]
    if not latent:
        out_specs.append(pl.BlockSpec((1, 2, 1, RET_DK, RET_DV), lambda b, h: (b, 0, h, 0, 0)))
        out_shape.append(jax.ShapeDtypeStruct((b_sz, 2, RET_HEADS, RET_DK, RET_DV), F32))
    return pl.pallas_call(
        functools.partial(_ret_kernel, latent=latent, n_chunks=n_chunks),
        grid=(b_sz, RET_HEADS), in_specs=in_specs, out_specs=out_specs, out_shape=out_shape,
        scratch_shapes=[pltpu.VMEM((n_chunks, RET_CHUNK, RET_DV), BF16),
                        pltpu.VMEM((n_chunks, RET_CHUNK, RET_DV), F32)],
        compiler_params=_params(2),
        name="retention_latent" if latent else "retention_context",
    )(*args)


def _ctxkeys_kernel(ckv_ref, kr_ref, w_kn, w_v, mkn_ref, mk_ref, mv_ref):
    ckvb = ckv_ref[0, 0].astype(BF16)
    kr = kr_ref[0, 0]
    k_nope = jnp.dot(ckvb, w_kn[...], preferred_element_type=F32)
    mv_ref[0, 0] = jnp.dot(ckvb, w_v[...], preferred_element_type=F32).astype(BF16)
    ss_kr = jnp.sum(kr * kr, axis=-1, keepdims=True)
    k_gain = mkn_ref[...]
    for hd in range(MLA_HEADS):
        kno = k_nope[:, hd * LANES:(hd + 1) * LANES]
        ss = jnp.sum(kno * kno, axis=-1, keepdims=True) + ss_kr
        rstd = lax.rsqrt(ss * (1.0 / MLA_QK) + EPS)
        mk_ref[0, 0, :, hd * MLA_PAD:hd * MLA_PAD + LANES] = (kno * rstd * k_gain[:, :LANES]).astype(BF16)
        mk_ref[0, 0, :, hd * MLA_PAD + LANES:(hd + 1) * MLA_PAD] = (kr * rstd * k_gain[:, LANES:]).astype(BF16)


def _context_keys(cache_ckv, cache_kr_pad, lws):
    b_sz, n_layers, p, _ = cache_ckv.shape
    outs = []
    for layer, lw in enumerate(lws):
        outs.append(pl.pallas_call(
            _ctxkeys_kernel,
            grid=(b_sz,),
            in_specs=[
                pl.BlockSpec((1, 1, p, MLA_KV_LORA), lambda b, layer=layer: (b, layer, 0, 0)),
                pl.BlockSpec((1, 1, p, LANES), lambda b, layer=layer: (b, layer, 0, 0)),
                _resident(lw["w_kn"].shape), _resident(lw["w_v"].shape), _resident(lw["mkn"].shape),
            ],
            out_specs=[pl.BlockSpec((1, 1, p, MLA_HEADS * MLA_PAD), lambda b: (b, 0, 0, 0)),
                       pl.BlockSpec((1, 1, p, MLA_HEADS * MLA_V), lambda b: (b, 0, 0, 0))],
            out_shape=[jax.ShapeDtypeStruct((b_sz, 1, p, MLA_HEADS * MLA_PAD), BF16),
                       jax.ShapeDtypeStruct((b_sz, 1, p, MLA_HEADS * MLA_V), BF16)],
            compiler_params=_params(1),
            name="mla_context_keys",
        )(cache_ckv, cache_kr_pad, lw["w_kn"], lw["w_v"], lw["mkn"]))
    return outs


def _post_kernel(x_ref, oa_ref, or_ref, om_ref, g_ref, mod_ref, n2_ref, wb_ref, wo_ref, wu_ref, wd_ref,
                 o_ref, *, ff_chunk):
    x = x_ref[0]
    mod = mod_ref[0, 0]
    d = x.shape[-1]
    merged = jnp.zeros(x.shape, F32)
    for i, br_ref in enumerate((oa_ref, or_ref, om_ref)):
        gate = _sigmoid(g_ref[0, :, i * d:(i + 1) * d].astype(F32))
        merged = merged + gate * jnp.dot(br_ref[0], wb_ref[i], preferred_element_type=F32)
    mix = jnp.dot(merged.astype(BF16), wo_ref[...], preferred_element_type=F32)
    x1 = x + mod[2:3] * mix
    h2 = (_rms(x1, d) * n2_ref[...] * (1.0 + mod[4:5]) + mod[3:4]).astype(BF16)
    acc = jnp.zeros(x.shape, F32)
    for c in range(D_FF // ff_chunk):
        u = jnp.dot(h2, wu_ref[:, c * ff_chunk:(c + 1) * ff_chunk], preferred_element_type=F32)
        u = jnp.square(jnp.maximum(u, 0.0)).astype(BF16)
        acc = acc + jnp.dot(u, wd_ref[c * ff_chunk:(c + 1) * ff_chunk, :], preferred_element_type=F32)
    o_ref[0] = x1 + mod[5:6] * acc


def _post(x, oa, orr, om, gates, mod, layer, lw, latent, tm):
    b_sz, n, d = x.shape
    tm = min(tm, n)

    def tok(width):
        return pl.BlockSpec((1, tm, width), lambda b, i: (b, i, 0))

    if latent:
        mod_spec = pl.BlockSpec((1, 1, 6, d), lambda b, i: (layer, 1 + b, 0, 0))
    else:
        mod_spec = pl.BlockSpec((1, 1, 6, d), lambda b, i: (layer, 0, 0, 0))
    weights = [lw["norm2"], lw["w_branch"], lw["w_out"], lw["w_up"], lw["w_down"]]
    return pl.pallas_call(
        functools.partial(_post_kernel, ff_chunk=1024),
        grid=(b_sz, n // tm),
        in_specs=[tok(d), tok(BRANCH_W), tok(BRANCH_W), tok(BRANCH_W), tok(N_BRANCH * d), mod_spec]
                 + [_resident(w.shape) for w in weights],
        out_specs=tok(d),
        out_shape=jax.ShapeDtypeStruct((b_sz, n, d), F32),
        compiler_params=_params(2),
        name="merge_mlp_latent" if latent else "merge_mlp_context",
    )(x, oa, orr, om, gates, mod, *weights)


def _rope_tables(n):
    pos = jnp.arange(n, dtype=jnp.int32)
    row = (pos // GRID_W).astype(F32)
    col = (pos % GRID_W).astype(F32)
    axis_dim = ROT_DIM // 2
    inv = ROPE_BASE ** (-jnp.arange(0, axis_dim, 2, dtype=F32) / axis_dim)
    ang_r = row[:, None] * inv[None, :]
    ang_c = col[:, None] * inv[None, :]
    zeros = jnp.zeros_like(ang_r)
    cos = jnp.concatenate([jnp.cos(ang_r)] * 2 + [jnp.cos(ang_c)] * 2, axis=-1)
    sin_a = jnp.concatenate([-jnp.sin(ang_r), zeros, -jnp.sin(ang_c), zeros], axis=-1)
    sin_b = jnp.concatenate([zeros, jnp.sin(ang_r), zeros, jnp.sin(ang_c)], axis=-1)
    reps = LANES // ROT_DIM
    return tuple(jnp.tile(t, (1, reps)) for t in (cos, sin_a, sin_b))


def _layer_weights(l, w_in, norm1, norm2, diff_qn, diff_kn, diff_lambda, diff_subln, ret_decay, ret_gn,
                   mla_qa_norm, w_mla_qb, mla_kva_norm, w_mla_kvb, mla_qn, mla_kn, w_branch, w_out, w_up,
                   w_down):
    d = w_in.shape[1]
    w = w_in[l]
    o_ret = 3 * BRANCH_W
    o_rk = o_ret + RET_HEADS * RET_DK
    o_rv = o_rk + RET_HEADS * RET_DK
    o_rg = o_rv + RET_HEADS * RET_DV
    o_qa = o_rg + RET_HEADS * RET_DV
    o_kva = o_qa + MLA_Q_LORA
    o_kr = o_kva + MLA_KV_LORA
    o_gate = o_kr + MLA_ROPE

    def dup_heads(cols):
        c = cols.reshape(d, RET_HEADS, RET_DK)
        return jnp.concatenate([c, c], axis=-1).reshape(d, RET_HEADS * 2 * RET_DK)

    w_ret = jnp.concatenate([dup_heads(w[:, o_ret:o_rk]), dup_heads(w[:, o_rk:o_rv]), w[:, o_rv:o_qa]], axis=-1)
    w_mla = jnp.concatenate([w[:, o_qa:o_gate], jnp.zeros((d, LANES - MLA_ROPE), w.dtype)], axis=-1)
    w_qb = jnp.pad(w_mla_qb[l].reshape(MLA_Q_LORA, MLA_HEADS, MLA_QK), ((0, 0), (0, 0), (0, MLA_PAD - MLA_QK)))
    w_kvb = w_mla_kvb[l].reshape(MLA_KV_LORA, MLA_HEADS, MLA_NOPE + MLA_V)

    def pad_gain(g):
        return jnp.pad(g, (0, MLA_PAD - MLA_QK))[None, :]

    return {
        "w_diff": w[:, :o_ret].astype(BF16),
        "w_ret": w_ret.astype(BF16),
        "w_mla": w_mla.astype(BF16),
        "w_gate": w[:, o_gate:].astype(BF16),
        "w_qb": w_qb.reshape(MLA_Q_LORA, MLA_HEADS * MLA_PAD).astype(BF16),
        "w_kn": w_kvb[:, :, :MLA_NOPE].reshape(MLA_KV_LORA, MLA_HEADS * MLA_NOPE).astype(BF16),
        "w_v": w_kvb[:, :, MLA_NOPE:].reshape(MLA_KV_LORA, MLA_HEADS * MLA_V).astype(BF16),
        "norm1": norm1[l][None, :], "norm2": norm2[l][None, :],
        "dqn": jnp.tile(diff_qn[l], LANES // DIFF_DH)[None, :],
        "dkn": jnp.tile(diff_kn[l], LANES // DIFF_DH)[None, :],
        "qa_g": mla_qa_norm[l][None, :], "kva_g": mla_kva_norm[l][None, :],
        "mqn": pad_gain(mla_qn[l]), "mkn": pad_gain(mla_kn[l]),
        "lmb": diff_lambda[l], "subln": diff_subln[l][None, :],
        "ret_decay": ret_decay[l], "ret_gn": ret_gn[l][None, :],
        "w_branch": w_branch[l].astype(BF16), "w_out": w_out[l].astype(BF16),
        "w_up": w_up[l].astype(BF16), "w_down": w_down[l].astype(BF16),
    }


def _lambda_init(l):
    a, b, c = LAMBDA_INIT_BASE
    return a - b * math.exp(-c * l)


def _block(x, mod, layer, lw, rope_tabs, caches, tm_in, tm_post, tq):
    latent = rope_tabs is not None
    outs = _inproj(x, mod, layer, lw, rope_tabs, tm_in)
    dq, dk, dv, rq, rk, rv, rg, mq, mk, mv, gates = outs[:11]
    lam_init = _lambda_init(layer)
    diff_cache = caches["diff"] if latent else None
    mla_cache = caches["mla"][layer] if latent else None
    oa = _attention(dq, dk, dv, diff_cache, layer, diff=True, lam_init=lam_init, lmb=lw["lmb"],
                    subln=lw["subln"], tq=tq)
    om = _attention(mq, mk, mv, mla_cache, 0, diff=False, tq=2 * tq)
    ret = _retention(rq, rk, rv, rg, lw["ret_decay"], lw["ret_gn"], caches["state"] if latent else None, layer)
    x = _post(x, oa, ret[0], om, gates, mod, layer, lw, latent, tm_post)
    if latent:
        return x, None
    return x, (outs[11], outs[12], outs[13], outs[14], ret[1])


def kernel(x_prompt, x_sample, cache_diff_k, cache_diff_v, cache_mla_ckv, cache_mla_krope, state_ret,
           c, c_ctx, w_mod, b_mod, norm1, norm2, w_in, diff_qn, diff_kn, diff_lambda, diff_subln,
           ret_decay, ret_gn, mla_qa_norm, w_mla_qb, mla_kva_norm, w_mla_kvb, mla_qn, mla_kn,
           w_branch, w_out, w_up, w_down):
    n_layers = w_in.shape[0]
    d = x_prompt.shape[-1]
    bd, n_lat = x_sample.shape[:2]
    b_ctx, n_ctx = x_prompt.shape[:2]
    p = cache_diff_k.shape[2]
    assert bd + 1 <= MOD_ROWS

    lws = [_layer_weights(l, w_in, norm1, norm2, diff_qn, diff_kn, diff_lambda, diff_subln, ret_decay, ret_gn,
                          mla_qa_norm, w_mla_qb, mla_kva_norm, w_mla_kvb, mla_qn, mla_kn, w_branch, w_out,
                          w_up, w_down) for l in range(n_layers)]

    cvec = jnp.concatenate([c_ctx[None, :], c, jnp.zeros((MOD_ROWS - 1 - bd, d), F32)], axis=0)
    mod = _modulation(cvec, w_mod, b_mod).reshape(n_layers, MOD_ROWS, 6, d)

    xp = x_prompt
    ctx_outs = []
    for l in range(n_layers):
        xp, ctx = _block(xp, mod, l, lws[l], None, None, 256, 256, 256)
        ctx_outs.append(ctx)
    new_diff_k = jnp.stack([o[0] for o in ctx_outs], axis=1).reshape(b_ctx, n_layers, n_ctx, 2 * DIFF_HEADS, DIFF_DH)
    new_diff_v = jnp.stack([o[1] for o in ctx_outs], axis=1).reshape(b_ctx, n_layers, n_ctx, DIFF_HEADS, 2 * DIFF_DH)
    new_mla_ckv = jnp.stack([o[2] for o in ctx_outs], axis=1)
    new_mla_krope = jnp.stack([o[3] for o in ctx_outs], axis=1)
    new_state_ret = jnp.stack([o[4] for o in ctx_outs], axis=1)

    caches = {
        "diff": (cache_diff_k.reshape(bd, n_layers, p, 2 * DIFF_HEADS * DIFF_DH).astype(BF16),
                 cache_diff_v.reshape(bd, n_layers, p, DIFF_HEADS * 2 * DIFF_DH).astype(BF16)),
        "mla": _context_keys(cache_mla_ckv, jnp.pad(cache_mla_krope, ((0, 0),) * 3 + ((0, LANES - MLA_ROPE),)), lws),
        "state": state_ret,
    }
    rope_tabs = _rope_tables(n_lat)
    xs = x_sample
    for l in range(n_layers):
        xs, _ = _block(xs, mod, l, lws[l], rope_tabs, caches, 256, 256, 256)

    return (xp, xs, new_diff_k, new_diff_v, new_mla_ckv, new_mla_krope, new_state_ret)
```

```python
import functools
import math

import jax
import jax.numpy as jnp
from jax import lax
from jax.experimental import pallas as pl
from jax.experimental.pallas import tpu as pltpu

F32 = jnp.float32
BF16 = jnp.bfloat16

D_MODEL = 1024
GRID_W = 64
DIFF_HEADS = 4
DIFF_DH = 64
RET_HEADS = 4
RET_DK = 64
RET_DV = 128
RET_CHUNK = 128
MLA_HEADS = 4
MLA_Q_LORA = 384
MLA_KV_LORA = 256
MLA_NOPE = 128
MLA_ROPE = 64
MLA_V = 128
MLA_QK = MLA_NOPE + MLA_ROPE
N_BRANCH = 3
BRANCH_W = 512
D_FF = 4 * D_MODEL
ROT_DIM = 64
ROPE_BASE = 10000.0
EPS = 1e-6
LAMBDA_INIT_BASE = (0.8, 0.6, 0.3)

LANES = 128
MLA_PAD = 2 * LANES
HEAD_W = 128
LOG2E = 1.4426950408889634
VMEM_LIMIT = 56 * 1024 * 1024
MOD_ROWS = 16


def _resident(shape):
    nd = len(shape)
    return pl.BlockSpec(shape, lambda *_: (0,) * nd, pipeline_mode=pl.Buffered(1))


def _params(n_grid):
    return pltpu.CompilerParams(dimension_semantics=("arbitrary",) * n_grid,
                                vmem_limit_bytes=VMEM_LIMIT)


def _sigmoid(x):
    return 1.0 / (1.0 + jnp.exp(-x))


def _rms(x, width):
    ss = jnp.sum(x * x, axis=-1, keepdims=True)
    return x * lax.rsqrt(ss * (1.0 / width) + EPS)


def _mod_kernel(c_ref, w_ref, b_ref, o_ref):
    c = c_ref[...]
    a = (c * _sigmoid(c)).astype(BF16)
    o_ref[0] = jnp.dot(a, w_ref[0].astype(BF16), preferred_element_type=F32) + b_ref[0]


def _modulation(cvec, w_mod, b_mod):
    n_layers, d, d6 = w_mod.shape
    tn = 1024
    return pl.pallas_call(
        _mod_kernel,
        grid=(n_layers, d6 // tn),
        in_specs=[
            pl.BlockSpec((MOD_ROWS, d), lambda l, j: (0, 0)),
            pl.BlockSpec((1, d, tn), lambda l, j: (l, 0, j)),
            pl.BlockSpec((1, 1, tn), lambda l, j: (l, 0, j)),
        ],
        out_specs=pl.BlockSpec((1, MOD_ROWS, tn), lambda l, j: (l, 0, j)),
        out_shape=jax.ShapeDtypeStruct((n_layers, MOD_ROWS, d6), F32),
        compiler_params=_params(2),
        name="modulation",
    )(cvec, w_mod, b_mod.reshape(n_layers, 1, d6))


def _inproj_kernel(*refs, latent):
    it = iter(refs)
    x_ref, mod_ref, n1_ref = next(it), next(it), next(it)
    if latent:
        cos_ref, sa_ref, sb_ref = next(it), next(it), next(it)
    w_diff, w_ret, w_mla, w_gate, w_qb, w_kn, w_v = (next(it) for _ in range(7))
    dqn_ref, dkn_ref, qag_ref, kvag_ref, mqn_ref, mkn_ref = (next(it) for _ in range(6))
    (dq_ref, dk_ref, dv_ref, rq_ref, rk_ref, rv_ref, rg_ref,
     mq_ref, mk_ref, mv_ref, gate_ref) = (next(it) for _ in range(11))
    if not latent:
        odk_ref, odv_ref, ockv_ref, okr_ref = (next(it) for _ in range(4))

    x = x_ref[0]
    mod = mod_ref[0, 0]
    h = _rms(x, D_MODEL) * n1_ref[...] * (1.0 + mod[1:2]) + mod[0:1]
    hb = h.astype(BF16)

    if latent:
        cos, sa, sb = cos_ref[...], sa_ref[...], sb_ref[...]

        def rope(y):
            return (y * cos + pltpu.roll(y, LANES - 16, 1) * sa + pltpu.roll(y, 16, 1) * sb)
    else:
        def rope(y):
            return y

    def group(z, g):
        return z[:, g * LANES:(g + 1) * LANES]

    r_i = lax.broadcasted_iota(jnp.int32, (LANES, LANES), 0) // DIFF_DH
    c_i = lax.broadcasted_iota(jnp.int32, (LANES, LANES), 1) // DIFF_DH
    seg = jnp.where(r_i == c_i, 1.0, 0.0).astype(BF16)

    def qk_norm(y, gain):
        ss = jnp.dot((y * y).astype(BF16), seg, preferred_element_type=F32)
        return y * lax.rsqrt(ss * (1.0 / DIFF_DH) + EPS) * gain

    zd = jnp.dot(hb, w_diff[...], preferred_element_type=F32)
    n_g = BRANCH_W // LANES
    q_gain = dqn_ref[...] * (DIFF_DH ** -0.5 * LOG2E)
    for g in range(n_g):
        sl = slice(g * LANES, (g + 1) * LANES)
        dq_ref[0, :, sl] = rope(qk_norm(group(zd, g), q_gain)).astype(BF16)
        kn = qk_norm(group(zd, n_g + g), dkn_ref[...])
        dk_ref[0, :, sl] = rope(kn).astype(BF16)
        v = group(zd, 2 * n_g + g)
        dv_ref[0, :, sl] = v.astype(BF16)
        if not latent:
            odk_ref[0, :, sl] = kn
            odv_ref[0, :, sl] = v

    zr = jnp.dot(hb, w_ret[...], preferred_element_type=F32)
    for g in range(n_g):
        sl = slice(g * LANES, (g + 1) * LANES)
        rq_ref[0, :, sl] = rope(group(zr, g)).astype(BF16)
        rk_ref[0, :, sl] = (rope(group(zr, n_g + g)) * RET_DK ** -0.5).astype(BF16)
        rv_ref[0, :, sl] = group(zr, 2 * n_g + g).astype(BF16)
        rg_ref[0, :, sl] = group(zr, 3 * n_g + g).astype(BF16)

    zm = jnp.dot(hb, w_mla[...], preferred_element_type=F32)
    qa = _rms(zm[:, :MLA_Q_LORA], MLA_Q_LORA) * qag_ref[...]
    mqz = jnp.dot(qa.astype(BF16), w_qb[...], preferred_element_type=F32)
    ckv = _rms(zm[:, MLA_Q_LORA:MLA_Q_LORA + MLA_KV_LORA], MLA_KV_LORA) * kvag_ref[...]
    ckvb = ckv.astype(BF16)
    k_nope = jnp.dot(ckvb, w_kn[...], preferred_element_type=F32)
    mv_ref[0] = jnp.dot(ckvb, w_v[...], preferred_element_type=F32).astype(BF16)
    kr = zm[:, MLA_Q_LORA + MLA_KV_LORA:]
    ss_kr = jnp.sum(kr * kr, axis=-1, keepdims=True)
    if not latent:
        ockv_ref[0] = ckv
        okr_ref[0] = kr[:, :MLA_ROPE]
    q_gain = mqn_ref[...] * (MLA_QK ** -0.5 * LOG2E)
    k_gain = mkn_ref[...]
    for hd in range(MLA_HEADS):
        lo = slice(hd * MLA_PAD, hd * MLA_PAD + LANES)
        hi = slice(hd * MLA_PAD + LANES, (hd + 1) * MLA_PAD)
        qh = _rms(mqz[:, hd * MLA_PAD:(hd + 1) * MLA_PAD], MLA_QK) * q_gain
        mq_ref[0, :, lo] = qh[:, :LANES].astype(BF16)
        mq_ref[0, :, hi] = rope(qh[:, LANES:]).astype(BF16)
        kno = k_nope[:, hd * LANES:(hd + 1) * LANES]
        ss = jnp.sum(kno * kno, axis=-1, keepdims=True) + ss_kr
        rstd = lax.rsqrt(ss * (1.0 / MLA_QK) + EPS)
        mk_ref[0, :, lo] = (kno * rstd * k_gain[:, :LANES]).astype(BF16)
        mk_ref[0, :, hi] = rope(kr * rstd * k_gain[:, LANES:]).astype(BF16)

    gate_ref[0] = jnp.dot(hb, w_gate[...], preferred_element_type=F32).astype(BF16)


def _inproj(x, mod, layer, lw, rope_tabs, tm):
    b_sz, n, d = x.shape
    latent = rope_tabs is not None
    tm = min(tm, n)
    grid = (b_sz, n // tm)

    def tok(width):
        return pl.BlockSpec((1, tm, width), lambda b, i: (b, i, 0))

    if latent:
        mod_spec = pl.BlockSpec((1, 1, 6, d), lambda b, i: (layer, 1 + b, 0, 0))
    else:
        mod_spec = pl.BlockSpec((1, 1, 6, d), lambda b, i: (layer, 0, 0, 0))
    in_specs = [tok(d), mod_spec, _resident((1, d))]
    args = [x, mod, lw["norm1"]]
    if latent:
        in_specs += [pl.BlockSpec((tm, LANES), lambda b, i: (i, 0))] * 3
        args += list(rope_tabs)
    weights = [lw["w_diff"], lw["w_ret"], lw["w_mla"], lw["w_gate"], lw["w_qb"], lw["w_kn"], lw["w_v"]]
    gains = [lw["dqn"], lw["dkn"], lw["qa_g"], lw["kva_g"], lw["mqn"], lw["mkn"]]
    for a in weights + gains:
        in_specs.append(_resident(a.shape))
        args.append(a)

    widths = [BRANCH_W] * 7 + [MLA_HEADS * MLA_PAD] * 2 + [MLA_HEADS * MLA_V, N_BRANCH * d]
    out_specs = [tok(w) for w in widths]
    out_shape = [jax.ShapeDtypeStruct((b_sz, n, w), BF16) for w in widths]
    if not latent:
        for w in (BRANCH_W, BRANCH_W, MLA_KV_LORA, MLA_ROPE):
            out_specs.append(tok(w))
            out_shape.append(jax.ShapeDtypeStruct((b_sz, n, w), F32))
    return pl.pallas_call(
        functools.partial(_inproj_kernel, latent=latent),
        grid=grid, in_specs=in_specs, out_specs=out_specs, out_shape=out_shape,
        compiler_params=_params(2),
        name="inproj_latent" if latent else "inproj_context",
    )(*args)


def _attn_kernel(*refs, diff, has_cache, n_new, n_past, tq, ck, lam_init):
    it = iter(refs)
    q_ref, kn_ref, vn_ref = next(it), next(it), next(it)
    if has_cache:
        kc_ref, vc_ref = next(it), next(it)
    if diff:
        lmb_ref, sub_ref = next(it), next(it)
    o_ref = next(it)
    s_bufs = (next(it), next(it))
    n_tiles = n_new // tq

    chunks = []
    for j in range(n_new // ck):
        chunks.append((lambda j=j: kn_ref[0, j * ck:(j + 1) * ck, :], lambda j=j: vn_ref[0, j * ck:(j + 1) * ck, :]))
    if has_cache:
        for j in range(n_past // ck):
            chunks.append((lambda j=j: kc_ref[0, 0, j * ck:(j + 1) * ck, :],
                           lambda j=j: vc_ref[0, 0, j * ck:(j + 1) * ck, :]))

    def tile(t):
        return pl.ds(pl.multiple_of(t * tq, tq), tq)

    rows = 2 * tq if diff else tq

    def stages(t_score, t_comb, m, slot):
        s_out = s_bufs[slot]
        s_in = s_bufs[slot if t_score is None else 1 - slot]
        if t_score is not None:
            q = q_ref[0, tile(t_score), :]
            if diff:
                lane = lax.broadcasted_iota(jnp.int32, q.shape, 1)
                qf = q.astype(F32)
                q = jnp.concatenate([jnp.where(lane < DIFF_DH, qf, 0.0), jnp.where(lane >= DIFF_DH, qf, 0.0)],
                                    axis=0).astype(BF16)
            m_acc = jnp.full((rows, LANES), -jnp.inf, F32)
        if t_comb is not None:
            l_acc = jnp.zeros((rows, LANES), F32)
            acc = jnp.zeros((rows, HEAD_W), F32)
        for j, (k_of, v_of) in enumerate(chunks):
            cols = slice(j * ck, (j + 1) * ck)
            if t_score is not None:
                s = lax.dot_general(q, k_of(), (((1,), (1,)), ((), ())), preferred_element_type=F32)
                s_out[:, cols] = s
                for u in range(ck // LANES):
                    m_acc = jnp.maximum(m_acc, s[:, u * LANES:(u + 1) * LANES])
            if t_comb is not None:
                e = jnp.exp2(s_in[:, cols] - m)
                for u in range(ck // LANES):
                    l_acc = l_acc + e[:, u * LANES:(u + 1) * LANES]
                acc = acc + jnp.dot(e.astype(BF16), v_of(), preferred_element_type=F32)
        if t_comb is not None:
            o = acc / jnp.sum(l_acc, axis=-1, keepdims=True)
            if diff:
                lmb = lmb_ref[...]
                lam = (jnp.exp(jnp.sum(lmb[0:1] * lmb[1:2], axis=-1, keepdims=True))
                       - jnp.exp(jnp.sum(lmb[2:3] * lmb[3:4], axis=-1, keepdims=True)) + lam_init)
                o = o[:tq] - lam * o[tq:]
                o = _rms(o, HEAD_W) * sub_ref[...] * (1.0 - lam_init)
            o_ref[0, tile(t_comb), :] = o.astype(BF16)
        if t_score is not None:
            return jnp.max(m_acc, axis=-1, keepdims=True)

    m_cur = stages(0, None, None, 0)
    if n_tiles > 1:
        assert n_tiles % 2 == 0

        def pair(p, m):
            m_odd = stages(2 * p + 1, 2 * p, m, 1)
            return stages(2 * p + 2, 2 * p + 1, m_odd, 0)

        m_cur = lax.fori_loop(0, n_tiles // 2 - 1, pair, m_cur)
        m_cur = stages(n_tiles - 1, n_tiles - 2, m_cur, 1)
    stages(None, n_tiles - 1, m_cur, (n_tiles - 1) % 2)


def _attention(q, k_new, v_new, cache, layer, *, diff, lam_init=0.0, lmb=None, subln=None, tq=256, ck=512):
    b_sz, n, _ = q.shape
    dq = LANES if diff else MLA_PAD
    n_heads = DIFF_HEADS if diff else MLA_HEADS
    tq = min(tq, n)
    has_cache = cache is not None
    n_past = cache[0].shape[2] if has_cache else 0
    ck = min(ck, n, n_past) if has_cache else min(ck, n)

    in_specs = [
        pl.BlockSpec((1, n, dq), lambda b, h: (b, 0, h)),
        pl.BlockSpec((1, n, dq), lambda b, h: (b, 0, h)),
        pl.BlockSpec((1, n, HEAD_W), lambda b, h: (b, 0, h)),
    ]
    args = [q, k_new, v_new]
    if has_cache:
        in_specs += [
            pl.BlockSpec((1, 1, n_past, dq), lambda b, h: (b, layer, 0, h)),
            pl.BlockSpec((1, 1, n_past, HEAD_W), lambda b, h: (b, layer, 0, h)),
        ]
        args += list(cache)
    if diff:
        in_specs += [_resident(lmb.shape), _resident(subln.shape)]
        args += [lmb, subln]
    rows = 2 * tq if diff else tq
    return pl.pallas_call(
        functools.partial(_attn_kernel, diff=diff, has_cache=has_cache, n_new=n, n_past=n_past,
                          tq=tq, ck=ck, lam_init=lam_init),
        grid=(b_sz, n_heads), in_specs=in_specs,
        out_specs=pl.BlockSpec((1, n, HEAD_W), lambda b, h: (b, 0, h)),
        out_shape=jax.ShapeDtypeStruct((b_sz, n, n_heads * HEAD_W), BF16),
        scratch_shapes=[pltpu.VMEM((rows, n + n_past), F32)] * 2,
        compiler_params=_params(2),
        name=("diff_attn" if diff else "mla_attn") + ("_latent" if has_cache else "_context"),
    )(*args)


def _log_sigmoid(x):
    return jnp.minimum(x, 0.0) - jnp.log(1.0 + jnp.exp(-jnp.abs(x)))


def _ret_kernel(*refs, latent, n_chunks):
    it = iter(refs)
    dec_ref, q_ref, k_ref, v_ref, g_ref, gn_ref = (next(it) for _ in range(6))
    if latent:
        r0_ref = next(it)
    o_ref = next(it)
    if not latent:
        st_ref = next(it)
    rcat_scr, u_scr = next(it), next(it)

    c_len = RET_CHUNK
    hd = pl.program_id(1)
    lg_f = _log_sigmoid(jnp.full((1, 1), dec_ref[0, hd], F32))
    lg_b = _log_sigmoid(jnp.full((1, 1), dec_ref[1, hd], F32))
    row = lax.broadcasted_iota(jnp.int32, (c_len, c_len), 0).astype(F32)
    col = lax.broadcasted_iota(jnp.int32, (c_len, c_len), 1).astype(F32)
    lane_fwd = lax.broadcasted_iota(jnp.int32, (c_len, 2 * RET_DK), 1) < RET_DK
    d_in = 0.5 * jnp.where(row >= col, jnp.exp((row - col) * lg_f), jnp.exp((col - row) * lg_b))
    q_dec = jnp.where(lane_fwd, jnp.exp((row + 1.0) * lg_f), jnp.exp((c_len - row) * lg_b))
    k_dec = jnp.where(lane_fwd, jnp.exp((c_len - 1.0 - row) * lg_f), jnp.exp(row * lg_b))
    dec_f = jnp.exp(c_len * lg_f)
    dec_b = jnp.exp(c_len * lg_b)

    def chunk(c):
        return pl.ds(pl.multiple_of(c * c_len, c_len), c_len)

    def inc_body(c, carry):
        kd = (k_ref[0, chunk(c), :].astype(F32) * k_dec).astype(BF16)
        u_scr[c] = lax.dot_general(kd, v_ref[0, chunk(c), :], (((0,), (0,)), ((), ())),
                                   preferred_element_type=F32)
        return carry

    unroll = min(4, n_chunks)
    lax.fori_loop(0, n_chunks, inc_body, 0, unroll=unroll)

    if latent:
        r_f0 = r0_ref[0, 0, 0, 0]
        r_b0 = r0_ref[0, 0, 1, 0]
    else:
        r_f0 = jnp.zeros((RET_DK, RET_DV), F32)
        r_b0 = r_f0

    def fwd_body(c, r):
        rcat_scr[c, 0:RET_DK, :] = r.astype(BF16)
        return dec_f * r + u_scr[c, 0:RET_DK, :]

    def bwd_body(t, r):
        c = n_chunks - 1 - t
        rcat_scr[c, RET_DK:2 * RET_DK, :] = r.astype(BF16)
        return dec_b * r + u_scr[c, RET_DK:2 * RET_DK, :]

    r_f = lax.fori_loop(0, n_chunks, fwd_body, r_f0)
    r_b = lax.fori_loop(0, n_chunks, bwd_body, r_b0)
    if not latent:
        st_ref[0, 0, 0] = r_f
        st_ref[0, 1, 0] = r_b

    gn = gn_ref[...]

    def out_body(c, carry):
        qc = q_ref[0, chunk(c), :]
        kc = k_ref[0, chunk(c), :]
        vc = v_ref[0, chunk(c), :]
        a = lax.dot_general(qc, kc, (((1,), (1,)), ((), ())), preferred_element_type=F32) * d_in
        qd = (qc.astype(F32) * q_dec).astype(BF16)
        o = (jnp.dot(a.astype(BF16), vc, preferred_element_type=F32)
             + jnp.dot(qd, rcat_scr[c], preferred_element_type=F32))
        gate = g_ref[0, chunk(c), :].astype(F32)
        y = _rms(o, RET_DV) * gn * (gate * _sigmoid(gate))
        o_ref[0, chunk(c), :] = y.astype(BF16)
        return carry

    lax.fori_loop(0, n_chunks, out_body, 0, unroll=unroll)


def _retention(rq, rk, rv, rg, ret_decay, ret_gn, state, layer):
    b_sz, n, _ = rq.shape
    latent = state is not None
    n_chunks = n // RET_CHUNK
    tokw = pl.BlockSpec((1, n, HEAD_W), lambda b, h: (b, 0, h))
    in_specs = [pl.BlockSpec(memory_space=pltpu.SMEM), tokw, tokw, tokw, tokw, _resident(ret_gn.shape)]
    args = [ret_decay, rq, rk, rv, rg, ret_gn]
    if latent:
        in_specs.append(pl.BlockSpec((1, 1, 2, 1, RET_DK, RET_DV), lambda b, h: (b, layer, 0, h, 0, 0)))
        args.append(state)
    out_specs = [tokw]
    out_shape = [jax.ShapeDtypeStruct((b_sz, n, RET_HEADS * HEAD_W), BF16)]
    if not latent:
        out_specs.append(pl.BlockSpec((1, 2, 1, RET_DK, RET_DV), lambda b, h: (b, 0, h, 0, 0)))
        out_shape.append(jax.ShapeDtypeStruct((b_sz, 2, RET_HEADS, RET_DK, RET_DV), F32))
    return pl.pallas_call(
        functools.partial(_ret_kernel, latent=latent, n_chunks=n_chunks),
        grid=(b_sz, RET_HEADS), in_specs=in_specs, out_specs=out_specs, out_shape=out_shape,
        scratch_shapes=[pltpu.VMEM((n_chunks, RET_CHUNK, RET_DV), BF16),
                        pltpu.VMEM((n_chunks, RET_CHUNK, RET_DV), F32)],
        compiler_params=_params(2),
        name="retention_latent" if latent else "retention_context",
    )(*args)


def _ctxkeys_kernel(ckv_ref, kr_ref, w_kn, w_v, mkn_ref, mk_ref, mv_ref):
    ckvb = ckv_ref[0, 0].astype(BF16)
    kr = kr_ref[0, 0]
    k_nope = jnp.dot(ckvb, w_kn[...], preferred_element_type=F32)
    mv_ref[0, 0] = jnp.dot(ckvb, w_v[...], preferred_element_type=F32).astype(BF16)
    ss_kr = jnp.sum(kr * kr, axis=-1, keepdims=True)
    k_gain = mkn_ref[...]
    for hd in range(MLA_HEADS):
        kno = k_nope[:, hd * LANES:(hd + 1) * LANES]
        ss = jnp.sum(kno * kno, axis=-1, keepdims=True) + ss_kr
        rstd = lax.rsqrt(ss * (1.0 / MLA_QK) + EPS)
        mk_ref[0, 0, :, hd * MLA_PAD:hd * MLA_PAD + LANES] = (kno * rstd * k_gain[:, :LANES]).astype(BF16)
        mk_ref[0, 0, :, hd * MLA_PAD + LANES:(hd + 1) * MLA_PAD] = (kr * rstd * k_gain[:, LANES:]).astype(BF16)


def _context_keys(cache_ckv, cache_kr_pad, lws):
    b_sz, n_layers, p, _ = cache_ckv.shape
    outs = []
    for layer, lw in enumerate(lws):
        outs.append(pl.pallas_call(
            _ctxkeys_kernel,
            grid=(b_sz,),
            in_specs=[
                pl.BlockSpec((1, 1, p, MLA_KV_LORA), lambda b, layer=layer: (b, layer, 0, 0)),
                pl.BlockSpec((1, 1, p, LANES), lambda b, layer=layer: (b, layer, 0, 0)),
                _resident(lw["w_kn"].shape), _resident(lw["w_v"].shape), _resident(lw["mkn"].shape),
            ],
            out_specs=[pl.BlockSpec((1, 1, p, MLA_HEADS * MLA_PAD), lambda b: (b, 0, 0, 0)),
                       pl.BlockSpec((1, 1, p, MLA_HEADS * MLA_V), lambda b: (b, 0, 0, 0))],
            out_shape=[jax.ShapeDtypeStruct((b_sz, 1, p, MLA_HEADS * MLA_PAD), BF16),
                       jax.ShapeDtypeStruct((b_sz, 1, p, MLA_HEADS * MLA_V), BF16)],
            compiler_params=_params(1),
            name="mla_context_keys",
        )(cache_ckv, cache_kr_pad, lw["w_kn"], lw["w_v"], lw["mkn"]))
    return outs


def _post_kernel(x_ref, oa_ref, or_ref, om_ref, g_ref, mod_ref, n2_ref, wb_ref, wo_ref, wu_ref, wd_ref,
                 o_ref, *, ff_chunk):
    x = x_ref[0]
    mod = mod_ref[0, 0]
    d = x.shape[-1]
    merged = jnp.zeros(x.shape, F32)
    for i, br_ref in enumerate((oa_ref, or_ref, om_ref)):
        gate = _sigmoid(g_ref[0, :, i * d:(i + 1) * d].astype(F32))
        merged = merged + gate * jnp.dot(br_ref[0], wb_ref[i], preferred_element_type=F32)
    mix = jnp.dot(merged.astype(BF16), wo_ref[...], preferred_element_type=F32)
    x1 = x + mod[2:3] * mix
    h2 = (_rms(x1, d) * n2_ref[...] * (1.0 + mod[4:5]) + mod[3:4]).astype(BF16)
    acc = jnp.zeros(x.shape, F32)
    for c in range(D_FF // ff_chunk):
        u = jnp.dot(h2, wu_ref[:, c * ff_chunk:(c + 1) * ff_chunk], preferred_element_type=F32)
        u = jnp.square(jnp.maximum(u, 0.0)).astype(BF16)
        acc = acc + jnp.dot(u, wd_ref[c * ff_chunk:(c + 1) * ff_chunk, :], preferred_element_type=F32)
    o_ref[0] = x1 + mod[5:6] * acc


def _post(x, oa, orr, om, gates, mod, layer, lw, latent, tm):
    b_sz, n, d = x.shape
    tm = min(tm, n)

    def tok(width):
        return pl.BlockSpec((1, tm, width), lambda b, i: (b, i, 0))

    if latent:
        mod_spec = pl.BlockSpec((1, 1, 6, d), lambda b, i: (layer, 1 + b, 0, 0))
    else:
        mod_spec = pl.BlockSpec((1, 1, 6, d), lambda b, i: (layer, 0, 0, 0))
    weights = [lw["norm2"], lw["w_branch"], lw["w_out"], lw["w_up"], lw["w_down"]]
    return pl.pallas_call(
        functools.partial(_post_kernel, ff_chunk=1024),
        grid=(b_sz, n // tm),
        in_specs=[tok(d), tok(BRANCH_W), tok(BRANCH_W), tok(BRANCH_W), tok(N_BRANCH * d), mod_spec]
                 + [_resident(w.shape) for w in weights],
        out_specs=tok(d),
        out_shape=jax.ShapeDtypeStruct((b_sz, n, d), F32),
        compiler_params=_params(2),
        name="merge_mlp_latent" if latent else "merge_mlp_context",
    )(x, oa, orr, om, gates, mod, *weights)


def _rope_tables(n):
    pos = jnp.arange(n, dtype=jnp.int32)
    row = (pos // GRID_W).astype(F32)
    col = (pos % GRID_W).astype(F32)
    axis_dim = ROT_DIM // 2
    inv = ROPE_BASE ** (-jnp.arange(0, axis_dim, 2, dtype=F32) / axis_dim)
    ang_r = row[:, None] * inv[None, :]
    ang_c = col[:, None] * inv[None, :]
    zeros = jnp.zeros_like(ang_r)
    cos = jnp.concatenate([jnp.cos(ang_r)] * 2 + [jnp.cos(ang_c)] * 2, axis=-1)
    sin_a = jnp.concatenate([-jnp.sin(ang_r), zeros, -jnp.sin(ang_c), zeros], axis=-1)
    sin_b = jnp.concatenate([zeros, jnp.sin(ang_r), zeros, jnp.sin(ang_c)], axis=-1)
    reps = LANES // ROT_DIM
    return tuple(jnp.tile(t, (1, reps)) for t in (cos, sin_a, sin_b))


def _layer_weights(l, w_in, norm1, norm2, diff_qn, diff_kn, diff_lambda, diff_subln, ret_decay, ret_gn,
                   mla_qa_norm, w_mla_qb, mla_kva_norm, w_mla_kvb, mla_qn, mla_kn, w_branch, w_out, w_up,
                   w_down):
    d = w_in.shape[1]
    w = w_in[l]
    o_ret = 3 * BRANCH_W
    o_rk = o_ret + RET_HEADS * RET_DK
    o_rv = o_rk + RET_HEADS * RET_DK
    o_rg = o_rv + RET_HEADS * RET_DV
    o_qa = o_rg + RET_HEADS * RET_DV
    o_kva = o_qa + MLA_Q_LORA
    o_kr = o_kva + MLA_KV_LORA
    o_gate = o_kr + MLA_ROPE

    def dup_heads(cols):
        c = cols.reshape(d, RET_HEADS, RET_DK)
        return jnp.concatenate([c, c], axis=-1).reshape(d, RET_HEADS * 2 * RET_DK)

    w_ret = jnp.concatenate([dup_heads(w[:, o_ret:o_rk]), dup_heads(w[:, o_rk:o_rv]), w[:, o_rv:o_qa]], axis=-1)
    w_mla = jnp.concatenate([w[:, o_qa:o_gate], jnp.zeros((d, LANES - MLA_ROPE), w.dtype)], axis=-1)
    w_qb = jnp.pad(w_mla_qb[l].reshape(MLA_Q_LORA, MLA_HEADS, MLA_QK), ((0, 0), (0, 0), (0, MLA_PAD - MLA_QK)))
    w_kvb = w_mla_kvb[l].reshape(MLA_KV_LORA, MLA_HEADS, MLA_NOPE + MLA_V)

    def pad_gain(g):
        return jnp.pad(g, (0, MLA_PAD - MLA_QK))[None, :]

    return {
        "w_diff": w[:, :o_ret].astype(BF16),
        "w_ret": w_ret.astype(BF16),
        "w_mla": w_mla.astype(BF16),
        "w_gate": w[:, o_gate:].astype(BF16),
        "w_qb": w_qb.reshape(MLA_Q_LORA, MLA_HEADS * MLA_PAD).astype(BF16),
        "w_kn": w_kvb[:, :, :MLA_NOPE].reshape(MLA_KV_LORA, MLA_HEADS * MLA_NOPE).astype(BF16),
        "w_v": w_kvb[:, :, MLA_NOPE:].reshape(MLA_KV_LORA, MLA_HEADS * MLA_V).astype(BF16),
        "norm1": norm1[l][None, :], "norm2": norm2[l][None, :],
        "dqn": jnp.tile(diff_qn[l], LANES // DIFF_DH)[None, :],
        "dkn": jnp.tile(diff_kn[l], LANES // DIFF_DH)[None, :],
        "qa_g": mla_qa_norm[l][None, :], "kva_g": mla_kva_norm[l][None, :],
        "mqn": pad_gain(mla_qn[l]), "mkn": pad_gain(mla_kn[l]),
        "lmb": diff_lambda[l], "subln": diff_subln[l][None, :],
        "ret_decay": ret_decay[l], "ret_gn": ret_gn[l][None, :],
        "w_branch": w_branch[l].astype(BF16), "w_out": w_out[l].astype(BF16),
        "w_up": w_up[l].astype(BF16), "w_down": w_down[l].astype(BF16),
    }


def _lambda_init(l):
    a, b, c = LAMBDA_INIT_BASE
    return a - b * math.exp(-c * l)


def _block(x, mod, layer, lw, rope_tabs, caches, tm_in, tm_post, tq):
    latent = rope_tabs is not None
    outs = _inproj(x, mod, layer, lw, rope_tabs, tm_in)
    dq, dk, dv, rq, rk, rv, rg, mq, mk, mv, gates = outs[:11]
    lam_init = _lambda_init(layer)
    diff_cache = caches["diff"] if latent else None
    mla_cache = caches["mla"][layer] if latent else None
    oa = _attention(dq, dk, dv, diff_cache, layer, diff=True, lam_init=lam_init, lmb=lw["lmb"],
                    subln=lw["subln"], tq=tq)
    om = _attention(mq, mk, mv, mla_cache, 0, diff=False, tq=2 * tq)
    ret = _retention(rq, rk, rv, rg, lw["ret_decay"], lw["ret_gn"], caches["state"] if latent else None, layer)
    x = _post(x, oa, ret[0], om, gates, mod, layer, lw, latent, tm_post)
    if latent:
        return x, None
    return x, (outs[11], outs[12], outs[13], outs[14], ret[1])


def kernel(x_prompt, x_sample, cache_diff_k, cache_diff_v, cache_mla_ckv, cache_mla_krope, state_ret,
           c, c_ctx, w_mod, b_mod, norm1, norm2, w_in, diff_qn, diff_kn, diff_lambda, diff_subln,
           ret_decay, ret_gn, mla_qa_norm, w_mla_qb, mla_kva_norm, w_mla_kvb, mla_qn, mla_kn,
           w_branch, w_out, w_up, w_down):
    n_layers = w_in.shape[0]
    d = x_prompt.shape[-1]
    bd, n_lat = x_sample.shape[:2]
    b_ctx, n_ctx = x_prompt.shape[:2]
    p = cache_diff_k.shape[2]
    assert bd + 1 <= MOD_ROWS

    lws = [_layer_weights(l, w_in, norm1, norm2, diff_qn, diff_kn, diff_lambda, diff_subln, ret_decay, ret_gn,
                          mla_qa_norm, w_mla_qb, mla_kva_norm, w_mla_kvb, mla_qn, mla_kn, w_branch, w_out,
                          w_up, w_down) for l in range(n_layers)]

    cvec = jnp.concatenate([c_ctx[None, :], c, jnp.zeros((MOD_ROWS - 1 - bd, d), F32)], axis=0)
    mod = _modulation(cvec, w_mod, b_mod).reshape(n_layers, MOD_ROWS, 6, d)

    xp = x_prompt
    ctx_outs = []
    for l in range(n_layers):
        xp, ctx = _block(xp, mod, l, lws[l], None, None, 256, 256, 256)
        ctx_outs.append(ctx)
    new_diff_k = jnp.stack([o[0] for o in ctx_outs], axis=1).reshape(b_ctx, n_layers, n_ctx, 2 * DIFF_HEADS, DIFF_DH)
    new_diff_v = jnp.stack([o[1] for o in ctx_outs], axis=1).reshape(b_ctx, n_layers, n_ctx, DIFF_HEADS, 2 * DIFF_DH)
    new_mla_ckv = jnp.stack([o[2] for o in ctx_outs], axis=1)
    new_mla_krope = jnp.stack([o[3] for o in ctx_outs], axis=1)
    new_state_ret = jnp.stack([o[4] for o in ctx_outs], axis=1)

    caches = {
        "diff": (cache_diff_k.reshape(bd, n_layers, p, 2 * DIFF_HEADS * DIFF_DH).astype(BF16),
                 cache_diff_v.reshape(bd, n_layers, p, DIFF_HEADS * 2 * DIFF_DH).astype(BF16)),
        "mla": _context_keys(cache_mla_ckv, jnp.pad(cache_mla_krope, ((0, 0),) * 3 + ((0, LANES - MLA_ROPE),)), lws),
        "state": state_ret,
    }
    rope_tabs = _rope_tables(n_lat)
    xs = x_sample
    for l in range(n_layers):
        xs, _ = _block(xs, mod, l, lws[l], rope_tabs, caches, 256, 256, 256)

    return (xp, xs, new_diff_k, new_diff_v, new_mla_ckv, new_mla_krope, new_state_ret)
```

```python
import functools
import math

import jax
import jax.numpy as jnp
from jax import lax
from jax.experimental import pallas as pl
from jax.experimental.pallas import tpu as pltpu

F32 = jnp.float32
BF16 = jnp.bfloat16

D_MODEL = 1024
GRID_W = 64
DIFF_HEADS = 4
DIFF_DH = 64
RET_HEADS = 4
RET_DK = 64
RET_DV = 128
RET_CHUNK = 128
MLA_HEADS = 4
MLA_Q_LORA = 384
MLA_KV_LORA = 256
MLA_NOPE = 128
MLA_ROPE = 64
MLA_V = 128
MLA_QK = MLA_NOPE + MLA_ROPE
N_BRANCH = 3
BRANCH_W = 512
D_FF = 4 * D_MODEL
ROT_DIM = 64
ROPE_BASE = 10000.0
EPS = 1e-6
LAMBDA_INIT_BASE = (0.8, 0.6, 0.3)

LANES = 128
MLA_PAD = 2 * LANES
HEAD_W = 128
LOG2E = 1.4426950408889634
VMEM_LIMIT = 56 * 1024 * 1024
MOD_ROWS = 16


def _resident(shape):
    nd = len(shape)
    return pl.BlockSpec(shape, lambda *_: (0,) * nd, pipeline_mode=pl.Buffered(1))


def _params(n_grid):
    return pltpu.CompilerParams(dimension_semantics=("arbitrary",) * n_grid,
                                vmem_limit_bytes=VMEM_LIMIT)


def _sigmoid(x):
    return 1.0 / (1.0 + jnp.exp(-x))


def _rms(x, width):
    ss = jnp.sum(x * x, axis=-1, keepdims=True)
    return x * lax.rsqrt(ss * (1.0 / width) + EPS)


def _mod_kernel(c_ref, w_ref, b_ref, o_ref):
    c = c_ref[...]
    a = (c * _sigmoid(c)).astype(BF16)
    o_ref[0] = jnp.dot(a, w_ref[0].astype(BF16), preferred_element_type=F32) + b_ref[0]


def _modulation(cvec, w_mod, b_mod):
    n_layers, d, d6 = w_mod.shape
    tn = 1024
    return pl.pallas_call(
        _mod_kernel,
        grid=(n_layers, d6 // tn),
        in_specs=[
            pl.BlockSpec((MOD_ROWS, d), lambda l, j: (0, 0)),
            pl.BlockSpec((1, d, tn), lambda l, j: (l, 0, j)),
            pl.BlockSpec((1, 1, tn), lambda l, j: (l, 0, j)),
        ],
        out_specs=pl.BlockSpec((1, MOD_ROWS, tn), lambda l, j: (l, 0, j)),
        out_shape=jax.ShapeDtypeStruct((n_layers, MOD_ROWS, d6), F32),
        compiler_params=_params(2),
        name="modulation",
    )(cvec, w_mod, b_mod.reshape(n_layers, 1, d6))


def _inproj_kernel(*refs, latent):
    it = iter(refs)
    x_ref, mod_ref, n1_ref = next(it), next(it), next(it)
    if latent:
        cos_ref, sa_ref, sb_ref = next(it), next(it), next(it)
    w_diff, w_ret, w_mla, w_gate, w_qb, w_kn, w_v = (next(it) for _ in range(7))
    dqn_ref, dkn_ref, qag_ref, kvag_ref, mqn_ref, mkn_ref = (next(it) for _ in range(6))
    (dq_ref, dk_ref, dv_ref, rq_ref, rk_ref, rv_ref, rg_ref,
     mq_ref, mk_ref, mv_ref, gate_ref) = (next(it) for _ in range(11))
    if not latent:
        odk_ref, odv_ref, ockv_ref, okr_ref = (next(it) for _ in range(4))

    x = x_ref[0]
    mod = mod_ref[0, 0]
    h = _rms(x, D_MODEL) * n1_ref[...] * (1.0 + mod[1:2]) + mod[0:1]
    hb = h.astype(BF16)

    if latent:
        cos, sa, sb = cos_ref[...], sa_ref[...], sb_ref[...]

        def rope(y):
            return (y * cos + pltpu.roll(y, LANES - 16, 1) * sa + pltpu.roll(y, 16, 1) * sb)
    else:
        def rope(y):
            return y

    def group(z, g):
        return z[:, g * LANES:(g + 1) * LANES]

    r_i = lax.broadcasted_iota(jnp.int32, (LANES, LANES), 0) // DIFF_DH
    c_i = lax.broadcasted_iota(jnp.int32, (LANES, LANES), 1) // DIFF_DH
    seg = jnp.where(r_i == c_i, 1.0, 0.0).astype(BF16)

    def qk_norm(y, gain):
        ss = jnp.dot((y * y).astype(BF16), seg, preferred_element_type=F32)
        return y * lax.rsqrt(ss * (1.0 / DIFF_DH) + EPS) * gain

    zd = jnp.dot(hb, w_diff[...], preferred_element_type=F32)
    n_g = BRANCH_W // LANES
    q_gain = dqn_ref[...] * (DIFF_DH ** -0.5 * LOG2E)
    for g in range(n_g):
        sl = slice(g * LANES, (g + 1) * LANES)
        dq_ref[0, :, sl] = rope(qk_norm(group(zd, g), q_gain)).astype(BF16)
        kn = qk_norm(group(zd, n_g + g), dkn_ref[...])
        dk_ref[0, :, sl] = rope(kn).astype(BF16)
        v = group(zd, 2 * n_g + g)
        dv_ref[0, :, sl] = v.astype(BF16)
        if not latent:
            odk_ref[0, :, sl] = kn
            odv_ref[0, :, sl] = v

    zr = jnp.dot(hb, w_ret[...], preferred_element_type=F32)
    for g in range(n_g):
        sl = slice(g * LANES, (g + 1) * LANES)
        rq_ref[0, :, sl] = rope(group(zr, g)).astype(BF16)
        rk_ref[0, :, sl] = (rope(group(zr, n_g + g)) * RET_DK ** -0.5).astype(BF16)
        rv_ref[0, :, sl] = group(zr, 2 * n_g + g).astype(BF16)
        rg_ref[0, :, sl] = group(zr, 3 * n_g + g).astype(BF16)

    zm = jnp.dot(hb, w_mla[...], preferred_element_type=F32)
    qa = _rms(zm[:, :MLA_Q_LORA], MLA_Q_LORA) * qag_ref[...]
    mqz = jnp.dot(qa.astype(BF16), w_qb[...], preferred_element_type=F32)
    ckv = _rms(zm[:, MLA_Q_LORA:MLA_Q_LORA + MLA_KV_LORA], MLA_KV_LORA) * kvag_ref[...]
    ckvb = ckv.astype(BF16)
    k_nope = jnp.dot(ckvb, w_kn[...], preferred_element_type=F32)
    mv_ref[0] = jnp.dot(ckvb, w_v[...], preferred_element_type=F32).astype(BF16)
    kr = zm[:, MLA_Q_LORA + MLA_KV_LORA:]
    ss_kr = jnp.sum(kr * kr, axis=-1, keepdims=True)
    if not latent:
        ockv_ref[0] = ckv
        okr_ref[0] = kr[:, :MLA_ROPE]
    q_gain = mqn_ref[...] * (MLA_QK ** -0.5 * LOG2E)
    k_gain = mkn_ref[...]
    for hd in range(MLA_HEADS):
        lo = slice(hd * MLA_PAD, hd * MLA_PAD + LANES)
        hi = slice(hd * MLA_PAD + LANES, (hd + 1) * MLA_PAD)
        qh = _rms(mqz[:, hd * MLA_PAD:(hd + 1) * MLA_PAD], MLA_QK) * q_gain
        mq_ref[0, :, lo] = qh[:, :LANES].astype(BF16)
        mq_ref[0, :, hi] = rope(qh[:, LANES:]).astype(BF16)
        kno = k_nope[:, hd * LANES:(hd + 1) * LANES]
        ss = jnp.sum(kno * kno, axis=-1, keepdims=True) + ss_kr
        rstd = lax.rsqrt(ss * (1.0 / MLA_QK) + EPS)
        mk_ref[0, :, lo] = (kno * rstd * k_gain[:, :LANES]).astype(BF16)
        mk_ref[0, :, hi] = rope(kr * rstd * k_gain[:, LANES:]).astype(BF16)

    gate_ref[0] = jnp.dot(hb, w_gate[...], preferred_element_type=F32).astype(BF16)


def _inproj(x, mod, layer, lw, rope_tabs, tm):
    b_sz, n, d = x.shape
    latent = rope_tabs is not None
    tm = min(tm, n)
    grid = (b_sz, n // tm)

    def tok(width):
        return pl.BlockSpec((1, tm, width), lambda b, i: (b, i, 0))

    if latent:
        mod_spec = pl.BlockSpec((1, 1, 6, d), lambda b, i: (layer, 1 + b, 0, 0))
    else:
        mod_spec = pl.BlockSpec((1, 1, 6, d), lambda b, i: (layer, 0, 0, 0))
    in_specs = [tok(d), mod_spec, _resident((1, d))]
    args = [x, mod, lw["norm1"]]
    if latent:
        in_specs += [pl.BlockSpec((tm, LANES), lambda b, i: (i, 0))] * 3
        args += list(rope_tabs)
    weights = [lw["w_diff"], lw["w_ret"], lw["w_mla"], lw["w_gate"], lw["w_qb"], lw["w_kn"], lw["w_v"]]
    gains = [lw["dqn"], lw["dkn"], lw["qa_g"], lw["kva_g"], lw["mqn"], lw["mkn"]]
    for a in weights + gains:
        in_specs.append(_resident(a.shape))
        args.append(a)

    widths = [BRANCH_W] * 7 + [MLA_HEADS * MLA_PAD] * 2 + [MLA_HEADS * MLA_V, N_BRANCH * d]
    out_specs = [tok(w) for w in widths]
    out_shape = [jax.ShapeDtypeStruct((b_sz, n, w), BF16) for w in widths]
    if not latent:
        for w in (BRANCH_W, BRANCH_W, MLA_KV_LORA, MLA_ROPE):
            out_specs.append(tok(w))
            out_shape.append(jax.ShapeDtypeStruct((b_sz, n, w), F32))
    return pl.pallas_call(
        functools.partial(_inproj_kernel, latent=latent),
        grid=grid, in_specs=in_specs, out_specs=out_specs, out_shape=out_shape,
        compiler_params=_params(2),
        name="inproj_latent" if latent else "inproj_context",
    )(*args)


def _attn_kernel(*refs, diff, has_cache, n_new, n_past, tq, ck, lam_init):
    it = iter(refs)
    q_ref, kn_ref, vn_ref = next(it), next(it), next(it)
    if has_cache:
        kc_ref, vc_ref = next(it), next(it)
    if diff:
        lmb_ref, sub_ref = next(it), next(it)
    o_ref = next(it)
    s_bufs = (next(it), next(it))
    n_tiles = n_new // tq

    chunks = []
    for j in range(n_new // ck):
        chunks.append((lambda j=j: kn_ref[0, j * ck:(j + 1) * ck, :], lambda j=j: vn_ref[0, j * ck:(j + 1) * ck, :]))
    if has_cache:
        for j in range(n_past // ck):
            chunks.append((lambda j=j: kc_ref[0, 0, j * ck:(j + 1) * ck, :].astype(BF16),
                           lambda j=j: vc_ref[0, 0, j * ck:(j + 1) * ck, :].astype(BF16)))

    def tile(t):
        return pl.ds(pl.multiple_of(t * tq, tq), tq)

    rows = 2 * tq if diff else tq

    def stages(t_score, t_comb, m, slot):
        s_out = s_bufs[slot]
        s_in = s_bufs[slot if t_score is None else 1 - slot]
        if t_score is not None:
            q = q_ref[0, tile(t_score), :]
            if diff:
                lane = lax.broadcasted_iota(jnp.int32, q.shape, 1)
                qf = q.astype(F32)
                q = jnp.concatenate([jnp.where(lane < DIFF_DH, qf, 0.0), jnp.where(lane >= DIFF_DH, qf, 0.0)],
                                    axis=0).astype(BF16)
            m_acc = jnp.full((rows, LANES), -jnp.inf, F32)
        if t_comb is not None:
            acc = jnp.zeros((rows, HEAD_W + LANES), F32)
            ones = jnp.ones((ck, LANES), BF16)
        for j, (k_of, v_of) in enumerate(chunks):
            cols = slice(j * ck, (j + 1) * ck)
            if t_score is not None:
                s = lax.dot_general(q, k_of(), (((1,), (1,)), ((), ())), preferred_element_type=F32)
                s_out[:, cols] = s
                for u in range(ck // LANES):
                    m_acc = jnp.maximum(m_acc, s[:, u * LANES:(u + 1) * LANES])
            if t_comb is not None:
                e = jnp.exp2(s_in[:, cols] - m)
                v1 = jnp.concatenate([v_of(), ones], axis=1)
                acc = acc + jnp.dot(e.astype(BF16), v1, preferred_element_type=F32)
        if t_comb is not None:
            o = acc[:, :HEAD_W] / acc[:, HEAD_W:]
            if diff:
                lmb = lmb_ref[...]
                lam = (jnp.exp(jnp.sum(lmb[0:1] * lmb[1:2], axis=-1, keepdims=True))
                       - jnp.exp(jnp.sum(lmb[2:3] * lmb[3:4], axis=-1, keepdims=True)) + lam_init)
                o = o[:tq] - lam * o[tq:]
                o = _rms(o, HEAD_W) * sub_ref[...] * (1.0 - lam_init)
            o_ref[0, tile(t_comb), :] = o.astype(BF16)
        if t_score is not None:
            return jnp.max(m_acc, axis=-1, keepdims=True)

    m_cur = stages(0, None, None, 0)
    if n_tiles > 1:
        assert n_tiles % 2 == 0

        def pair(p, m):
            m_odd = stages(2 * p + 1, 2 * p, m, 1)
            return stages(2 * p + 2, 2 * p + 1, m_odd, 0)

        m_cur = lax.fori_loop(0, n_tiles // 2 - 1, pair, m_cur)
        m_cur = stages(n_tiles - 1, n_tiles - 2, m_cur, 1)
    stages(None, n_tiles - 1, m_cur, (n_tiles - 1) % 2)


def _attention(q, k_new, v_new, cache, layer, *, diff, lam_init=0.0, lmb=None, subln=None, tq=256, ck=512):
    b_sz, n, _ = q.shape
    dq = LANES if diff else MLA_PAD
    n_heads = DIFF_HEADS if diff else MLA_HEADS
    tq = min(tq, n)
    has_cache = cache is not None
    n_past = cache[0].shape[2] if has_cache else 0
    ck = min(ck, n, n_past) if has_cache else min(ck, n)

    in_specs = [
        pl.BlockSpec((1, n, dq), lambda b, h: (b, 0, h)),
        pl.BlockSpec((1, n, dq), lambda b, h: (b, 0, h)),
        pl.BlockSpec((1, n, HEAD_W), lambda b, h: (b, 0, h)),
    ]
    args = [q, k_new, v_new]
    if has_cache:
        in_specs += [
            pl.BlockSpec((1, 1, n_past, dq), lambda b, h: (b, layer, 0, h)),
            pl.BlockSpec((1, 1, n_past, HEAD_W), lambda b, h: (b, layer, 0, h)),
        ]
        args += list(cache)
    if diff:
        in_specs += [_resident(lmb.shape), _resident(subln.shape)]
        args += [lmb, subln]
    rows = 2 * tq if diff else tq
    return pl.pallas_call(
        functools.partial(_attn_kernel, diff=diff, has_cache=has_cache, n_new=n, n_past=n_past,
                          tq=tq, ck=ck, lam_init=lam_init),
        grid=(b_sz, n_heads), in_specs=in_specs,
        out_specs=pl.BlockSpec((1, n, HEAD_W), lambda b, h: (b, 0, h)),
        out_shape=jax.ShapeDtypeStruct((b_sz, n, n_heads * HEAD_W), BF16),
        scratch_shapes=[pltpu.VMEM((rows, n + n_past), F32)] * 2,
        compiler_params=_params(2),
        name=("diff_attn" if diff else "mla_attn") + ("_latent" if has_cache else "_context"),
    )(*args)


def _log_sigmoid(x):
    return jnp.minimum(x, 0.0) - jnp.log(1.0 + jnp.exp(-jnp.abs(x)))


def _ret_kernel(*refs, latent, n_chunks):
    it = iter(refs)
    dec_ref, q_ref, k_ref, v_ref, g_ref, gn_ref = (next(it) for _ in range(6))
    if latent:
        r0_ref = next(it)
    o_ref = next(it)
    if not latent:
        st_ref = next(it)
    rcat_scr, u_scr = next(it), next(it)

    c_len = RET_CHUNK
    hd = pl.program_id(1)
    lg_f = _log_sigmoid(jnp.full((1, 1), dec_ref[0, hd], F32))
    lg_b = _log_sigmoid(jnp.full((1, 1), dec_ref[1, hd], F32))
    row = lax.broadcasted_iota(jnp.int32, (c_len, c_len), 0).astype(F32)
    col = lax.broadcasted_iota(jnp.int32, (c_len, c_len), 1).astype(F32)
    lane_fwd = lax.broadcasted_iota(jnp.int32, (c_len, 2 * RET_DK), 1) < RET_DK
    d_in = 0.5 * jnp.where(row >= col, jnp.exp((row - col) * lg_f), jnp.exp((col - row) * lg_b))
    q_dec = jnp.where(lane_fwd, jnp.exp((row + 1.0) * lg_f), jnp.exp((c_len - row) * lg_b))
    k_dec = jnp.where(lane_fwd, jnp.exp((c_len - 1.0 - row) * lg_f), jnp.exp(row * lg_b))
    dec_f = jnp.exp(c_len * lg_f)
    dec_b = jnp.exp(c_len * lg_b)

    def chunk(c):
        return pl.ds(pl.multiple_of(c * c_len, c_len), c_len)

    def inc_body(c, carry):
        kd = (k_ref[0, chunk(c), :].astype(F32) * k_dec).astype(BF16)
        u_scr[c] = lax.dot_general(kd, v_ref[0, chunk(c), :], (((0,), (0,)), ((), ())),
                                   preferred_element_type=F32)
        return carry

    unroll = min(4, n_chunks)
    lax.fori_loop(0, n_chunks, inc_body, 0, unroll=unroll)

    if latent:
        r_f0 = r0_ref[0, 0, 0, 0]
        r_b0 = r0_ref[0, 0, 1, 0]
    else:
        r_f0 = jnp.zeros((RET_DK, RET_DV), F32)
        r_b0 = r_f0

    def fwd_body(c, r):
        rcat_scr[c, 0:RET_DK, :] = r.astype(BF16)
        return dec_f * r + u_scr[c, 0:RET_DK, :]

    def bwd_body(t, r):
        c = n_chunks - 1 - t
        rcat_scr[c, RET_DK:2 * RET_DK, :] = r.astype(BF16)
        return dec_b * r + u_scr[c, RET_DK:2 * RET_DK, :]

    r_f = lax.fori_loop(0, n_chunks, fwd_body, r_f0)
    r_b = lax.fori_loop(0, n_chunks, bwd_body, r_b0)
    if not latent:
        st_ref[0, 0, 0] = r_f
        st_ref[0, 1, 0] = r_b

    gn = gn_ref[...]

    def out_body(c, carry):
        qc = q_ref[0, chunk(c), :]
        kc = k_ref[0, chunk(c), :]
        vc = v_ref[0, chunk(c), :]
        a = lax.dot_general(qc, kc, (((1,), (1,)), ((), ())), preferred_element_type=F32) * d_in
        qd = (qc.astype(F32) * q_dec).astype(BF16)
        o = (jnp.dot(a.astype(BF16), vc, preferred_element_type=F32)
             + jnp.dot(qd, rcat_scr[c], preferred_element_type=F32))
        gate = g_ref[0, chunk(c), :].astype(F32)
        y = _rms(o, RET_DV) * gn * (gate * _sigmoid(gate))
        o_ref[0, chunk(c), :] = y.astype(BF16)
        return carry

    lax.fori_loop(0, n_chunks, out_body, 0, unroll=unroll)


def _retention(rq, rk, rv, rg, ret_decay, ret_gn, state, layer):
    b_sz, n, _ = rq.shape
    latent = state is not None
    n_chunks = n // RET_CHUNK
    tokw = pl.BlockSpec((1, n, HEAD_W), lambda b, h: (b, 0, h))
    in_specs = [pl.BlockSpec(memory_space=pltpu.SMEM), tokw, tokw, tokw, tokw, _resident(ret_gn.shape)]
    args = [ret_decay, rq, rk, rv, rg, ret_gn]
    if latent:
        in_specs.append(pl.BlockSpec((1, 1, 2, 1, RET_DK, RET_DV), lambda b, h: (b, layer, 0, h, 0, 0)))
        args.append(state)
    out_specs = [tokw]
    out_shape = [jax.ShapeDtypeStruct((b_sz, n, RET_HEADS * HEAD_W), BF16)]
    if not latent:
        out_specs.append(pl.BlockSpec((1, 2, 1, RET_DK, RET_DV), lambda b, h: (b, 0, h, 0, 0)))
        out_shape.append(jax.ShapeDtypeStruct((b_sz, 2, RET_HEADS, RET_DK, RET_DV), F32))
    return pl.pallas_call(
        functools.partial(_ret_kernel, latent=latent, n_chunks=n_chunks),
        grid=(b_sz, RET_HEADS), in_specs=in_specs, out_specs=out_specs, out_shape=out_shape,
        scratch_shapes=[pltpu.VMEM((n_chunks, RET_CHUNK, RET_DV), BF16),
                        pltpu.VMEM((n_chunks, RET_CHUNK, RET_DV), F32)],
        compiler_params=_params(2),
        name="retention_latent" if latent else "retention_context",
    )(*args)


def _ctxkeys_kernel(ckv_ref, kr_ref, w_kn, w_v, mkn_ref, mk_ref, mv_ref):
    ckvb = ckv_ref[0, 0].astype(BF16)
    kr = kr_ref[0, 0]
    k_nope = jnp.dot(ckvb, w_kn[...], preferred_element_type=F32)
    mv_ref[0, 0] = jnp.dot(ckvb, w_v[...], preferred_element_type=F32).astype(BF16)
    ss_kr = jnp.sum(kr * kr, axis=-1, keepdims=True)
    k_gain = mkn_ref[...]
    for hd in range(MLA_HEADS):
        kno = k_nope[:, hd * LANES:(hd + 1) * LANES]
        ss = jnp.sum(kno * kno, axis=-1, keepdims=True) + ss_kr
        rstd = lax.rsqrt(ss * (1.0 / MLA_QK) + EPS)
        mk_ref[0, 0, :, hd * MLA_PAD:hd * MLA_PAD + LANES] = (kno * rstd * k_gain[:, :LANES]).astype(BF16)
        mk_ref[0, 0, :, hd * MLA_PAD + LANES:(hd + 1) * MLA_PAD] = (kr * rstd * k_gain[:, LANES:]).astype(BF16)


def _context_keys(cache_ckv, cache_kr_pad, lws):
    b_sz, n_layers, p, _ = cache_ckv.shape
    outs = []
    for layer, lw in enumerate(lws):
        outs.append(pl.pallas_call(
            _ctxkeys_kernel,
            grid=(b_sz,),
            in_specs=[
                pl.BlockSpec((1, 1, p, MLA_KV_LORA), lambda b, layer=layer: (b, layer, 0, 0)),
                pl.BlockSpec((1, 1, p, LANES), lambda b, layer=layer: (b, layer, 0, 0)),
                _resident(lw["w_kn"].shape), _resident(lw["w_v"].shape), _resident(lw["mkn"].shape),
            ],
            out_specs=[pl.BlockSpec((1, 1, p, MLA_HEADS * MLA_PAD), lambda b: (b, 0, 0, 0)),
                       pl.BlockSpec((1, 1, p, MLA_HEADS * MLA_V), lambda b: (b, 0, 0, 0))],
            out_shape=[jax.ShapeDtypeStruct((b_sz, 1, p, MLA_HEADS * MLA_PAD), BF16),
                       jax.ShapeDtypeStruct((b_sz, 1, p, MLA_HEADS * MLA_V), BF16)],
            compiler_params=_params(1),
            name="mla_context_keys",
        )(cache_ckv, cache_kr_pad, lw["w_kn"], lw["w_v"], lw["mkn"]))
    return outs


def _post_kernel(x_ref, oa_ref, or_ref, om_ref, g_ref, mod_ref, n2_ref, wb_ref, wo_ref, wu_ref, wd_ref,
                 o_ref, *, ff_chunk):
    x = x_ref[0]
    mod = mod_ref[0, 0]
    d = x.shape[-1]
    merged = jnp.zeros(x.shape, F32)
    for i, br_ref in enumerate((oa_ref, or_ref, om_ref)):
        gate = _sigmoid(g_ref[0, :, i * d:(i + 1) * d].astype(F32))
        merged = merged + gate * jnp.dot(br_ref[0], wb_ref[i], preferred_element_type=F32)
    mix = jnp.dot(merged.astype(BF16), wo_ref[...], preferred_element_type=F32)
    x1 = x + mod[2:3] * mix
    h2 = (_rms(x1, d) * n2_ref[...] * (1.0 + mod[4:5]) + mod[3:4]).astype(BF16)
    acc = jnp.zeros(x.shape, F32)
    for c in range(D_FF // ff_chunk):
        u = jnp.dot(h2, wu_ref[:, c * ff_chunk:(c + 1) * ff_chunk], preferred_element_type=F32)
        u = jnp.square(jnp.maximum(u, 0.0)).astype(BF16)
        acc = acc + jnp.dot(u, wd_ref[c * ff_chunk:(c + 1) * ff_chunk, :], preferred_element_type=F32)
    o_ref[0] = x1 + mod[5:6] * acc


def _post(x, oa, orr, om, gates, mod, layer, lw, latent, tm):
    b_sz, n, d = x.shape
    tm = min(tm, n)

    def tok(width):
        return pl.BlockSpec((1, tm, width), lambda b, i: (b, i, 0))

    if latent:
        mod_spec = pl.BlockSpec((1, 1, 6, d), lambda b, i: (layer, 1 + b, 0, 0))
    else:
        mod_spec = pl.BlockSpec((1, 1, 6, d), lambda b, i: (layer, 0, 0, 0))
    weights = [lw["norm2"], lw["w_branch"], lw["w_out"], lw["w_up"], lw["w_down"]]
    return pl.pallas_call(
        functools.partial(_post_kernel, ff_chunk=1024),
        grid=(b_sz, n // tm),
        in_specs=[tok(d), tok(BRANCH_W), tok(BRANCH_W), tok(BRANCH_W), tok(N_BRANCH * d), mod_spec]
                 + [_resident(w.shape) for w in weights],
        out_specs=tok(d),
        out_shape=jax.ShapeDtypeStruct((b_sz, n, d), F32),
        compiler_params=_params(2),
        name="merge_mlp_latent" if latent else "merge_mlp_context",
    )(x, oa, orr, om, gates, mod, *weights)


def _rope_tables(n):
    pos = jnp.arange(n, dtype=jnp.int32)
    row = (pos // GRID_W).astype(F32)
    col = (pos % GRID_W).astype(F32)
    axis_dim = ROT_DIM // 2
    inv = ROPE_BASE ** (-jnp.arange(0, axis_dim, 2, dtype=F32) / axis_dim)
    ang_r = row[:, None] * inv[None, :]
    ang_c = col[:, None] * inv[None, :]
    zeros = jnp.zeros_like(ang_r)
    cos = jnp.concatenate([jnp.cos(ang_r)] * 2 + [jnp.cos(ang_c)] * 2, axis=-1)
    sin_a = jnp.concatenate([-jnp.sin(ang_r), zeros, -jnp.sin(ang_c), zeros], axis=-1)
    sin_b = jnp.concatenate([zeros, jnp.sin(ang_r), zeros, jnp.sin(ang_c)], axis=-1)
    reps = LANES // ROT_DIM
    return tuple(jnp.tile(t, (1, reps)) for t in (cos, sin_a, sin_b))


def _layer_weights(l, w_in, norm1, norm2, diff_qn, diff_kn, diff_lambda, diff_subln, ret_decay, ret_gn,
                   mla_qa_norm, w_mla_qb, mla_kva_norm, w_mla_kvb, mla_qn, mla_kn, w_branch, w_out, w_up,
                   w_down):
    d = w_in.shape[1]
    w = w_in[l]
    o_ret = 3 * BRANCH_W
    o_rk = o_ret + RET_HEADS * RET_DK
    o_rv = o_rk + RET_HEADS * RET_DK
    o_rg = o_rv + RET_HEADS * RET_DV
    o_qa = o_rg + RET_HEADS * RET_DV
    o_kva = o_qa + MLA_Q_LORA
    o_kr = o_kva + MLA_KV_LORA
    o_gate = o_kr + MLA_ROPE

    def dup_heads(cols):
        c = cols.reshape(d, RET_HEADS, RET_DK)
        return jnp.concatenate([c, c], axis=-1).reshape(d, RET_HEADS * 2 * RET_DK)

    w_ret = jnp.concatenate([dup_heads(w[:, o_ret:o_rk]), dup_heads(w[:, o_rk:o_rv]), w[:, o_rv:o_qa]], axis=-1)
    w_mla = jnp.concatenate([w[:, o_qa:o_gate], jnp.zeros((d, LANES - MLA_ROPE), w.dtype)], axis=-1)
    w_qb = jnp.pad(w_mla_qb[l].reshape(MLA_Q_LORA, MLA_HEADS, MLA_QK), ((0, 0), (0, 0), (0, MLA_PAD - MLA_QK)))
    w_kvb = w_mla_kvb[l].reshape(MLA_KV_LORA, MLA_HEADS, MLA_NOPE + MLA_V)

    def pad_gain(g):
        return jnp.pad(g, (0, MLA_PAD - MLA_QK))[None, :]

    return {
        "w_diff": w[:, :o_ret].astype(BF16),
        "w_ret": w_ret.astype(BF16),
        "w_mla": w_mla.astype(BF16),
        "w_gate": w[:, o_gate:].astype(BF16),
        "w_qb": w_qb.reshape(MLA_Q_LORA, MLA_HEADS * MLA_PAD).astype(BF16),
        "w_kn": w_kvb[:, :, :MLA_NOPE].reshape(MLA_KV_LORA, MLA_HEADS * MLA_NOPE).astype(BF16),
        "w_v": w_kvb[:, :, MLA_NOPE:].reshape(MLA_KV_LORA, MLA_HEADS * MLA_V).astype(BF16),
        "norm1": norm1[l][None, :], "norm2": norm2[l][None, :],
        "dqn": jnp.tile(diff_qn[l], LANES // DIFF_DH)[None, :],
        "dkn": jnp.tile(diff_kn[l], LANES // DIFF_DH)[None, :],
        "qa_g": mla_qa_norm[l][None, :], "kva_g": mla_kva_norm[l][None, :],
        "mqn": pad_gain(mla_qn[l]), "mkn": pad_gain(mla_kn[l]),
        "lmb": diff_lambda[l], "subln": diff_subln[l][None, :],
        "ret_decay": ret_decay[l], "ret_gn": ret_gn[l][None, :],
        "w_branch": w_branch[l].astype(BF16), "w_out": w_out[l].astype(BF16),
        "w_up": w_up[l].astype(BF16), "w_down": w_down[l].astype(BF16),
    }


def _lambda_init(l):
    a, b, c = LAMBDA_INIT_BASE
    return a - b * math.exp(-c * l)


def _block(x, mod, layer, lw, rope_tabs, caches, tm_in, tm_post, tq):
    latent = rope_tabs is not None
    outs = _inproj(x, mod, layer, lw, rope_tabs, tm_in)
    dq, dk, dv, rq, rk, rv, rg, mq, mk, mv, gates = outs[:11]
    lam_init = _lambda_init(layer)
    diff_cache = caches["diff"] if latent else None
    mla_cache = caches["mla"][layer] if latent else None
    oa = _attention(dq, dk, dv, diff_cache, layer, diff=True, lam_init=lam_init, lmb=lw["lmb"],
                    subln=lw["subln"], tq=tq)
    om = _attention(mq, mk, mv, mla_cache, 0, diff=False, tq=2 * tq)
    ret = _retention(rq, rk, rv, rg, lw["ret_decay"], lw["ret_gn"], caches["state"] if latent else None, layer)
    x = _post(x, oa, ret[0], om, gates, mod, layer, lw, latent, tm_post)
    if latent:
        return x, None
    return x, (outs[11], outs[12], outs[13], outs[14], ret[1])


def kernel(x_prompt, x_sample, cache_diff_k, cache_diff_v, cache_mla_ckv, cache_mla_krope, state_ret,
           c, c_ctx, w_mod, b_mod, norm1, norm2, w_in, diff_qn, diff_kn, diff_lambda, diff_subln,
           ret_decay, ret_gn, mla_qa_norm, w_mla_qb, mla_kva_norm, w_mla_kvb, mla_qn, mla_kn,
           w_branch, w_out, w_up, w_down):
    n_layers = w_in.shape[0]
    d = x_prompt.shape[-1]
    bd, n_lat = x_sample.shape[:2]
    b_ctx, n_ctx = x_prompt.shape[:2]
    p = cache_diff_k.shape[2]
    assert bd + 1 <= MOD_ROWS

    lws = [_layer_weights(l, w_in, norm1, norm2, diff_qn, diff_kn, diff_lambda, diff_subln, ret_decay, ret_gn,
                          mla_qa_norm, w_mla_qb, mla_kva_norm, w_mla_kvb, mla_qn, mla_kn, w_branch, w_out,
                          w_up, w_down) for l in range(n_layers)]

    cvec = jnp.concatenate([c_ctx[None, :], c, jnp.zeros((MOD_ROWS - 1 - bd, d), F32)], axis=0)
    mod = _modulation(cvec, w_mod, b_mod).reshape(n_layers, MOD_ROWS, 6, d)

    xp = x_prompt
    ctx_outs = []
    for l in range(n_layers):
        xp, ctx = _block(xp, mod, l, lws[l], None, None, 256, 256, 256)
        ctx_outs.append(ctx)
    new_diff_k = jnp.stack([o[0] for o in ctx_outs], axis=1).reshape(b_ctx, n_layers, n_ctx, 2 * DIFF_HEADS, DIFF_DH)
    new_diff_v = jnp.stack([o[1] for o in ctx_outs], axis=1).reshape(b_ctx, n_layers, n_ctx, DIFF_HEADS, 2 * DIFF_DH)
    new_mla_ckv = jnp.stack([o[2] for o in ctx_outs], axis=1)
    new_mla_krope = jnp.stack([o[3] for o in ctx_outs], axis=1)
    new_state_ret = jnp.stack([o[4] for o in ctx_outs], axis=1)

    caches = {
        "diff": (cache_diff_k.reshape(bd, n_layers, p, 2 * DIFF_HEADS * DIFF_DH),
                 cache_diff_v.reshape(bd, n_layers, p, DIFF_HEADS * 2 * DIFF_DH)),
        "mla": _context_keys(cache_mla_ckv, jnp.pad(cache_mla_krope, ((0, 0),) * 3 + ((0, LANES - MLA_ROPE),)), lws),
        "state": state_ret,
    }
    rope_tabs = _rope_tables(n_lat)
    xs = x_sample
    for l in range(n_layers):
        xs, _ = _block(xs, mod, l, lws[l], rope_tabs, caches, 512, 512, 256)

    return (xp, xs, new_diff_k, new_diff_v, new_mla_ckv, new_mla_krope, new_state_ret)
```

```python
import functools
import math

import jax
import jax.numpy as jnp
from jax import lax
from jax.experimental import pallas as pl
from jax.experimental.pallas import tpu as pltpu

F32 = jnp.float32
BF16 = jnp.bfloat16

D_MODEL = 1024
GRID_W = 64
DIFF_HEADS = 4
DIFF_DH = 64
RET_HEADS = 4
RET_DK = 64
RET_DV = 128
RET_CHUNK = 128
MLA_HEADS = 4
MLA_Q_LORA = 384
MLA_KV_LORA = 256
MLA_NOPE = 128
MLA_ROPE = 64
MLA_V = 128
MLA_QK = MLA_NOPE + MLA_ROPE
N_BRANCH = 3
BRANCH_W = 512
D_FF = 4 * D_MODEL
ROT_DIM = 64
ROPE_BASE = 10000.0
EPS = 1e-6
LAMBDA_INIT_BASE = (0.8, 0.6, 0.3)

LANES = 128
MLA_PAD = 2 * LANES
HEAD_W = 128
LOG2E = 1.4426950408889634
VMEM_LIMIT = 56 * 1024 * 1024
_IDLE = object()
MOD_ROWS = 16


def _resident(shape):
    nd = len(shape)
    return pl.BlockSpec(shape, lambda *_: (0,) * nd, pipeline_mode=pl.Buffered(1))


def _params(n_grid):
    return pltpu.CompilerParams(dimension_semantics=("arbitrary",) * n_grid,
                                vmem_limit_bytes=VMEM_LIMIT)


def _sigmoid(x):
    return 1.0 / (1.0 + jnp.exp(-x))


def _rms(x, width):
    ss = jnp.sum(x * x, axis=-1, keepdims=True)
    return x * lax.rsqrt(ss * (1.0 / width) + EPS)


def _mod_kernel(c_ref, w_ref, b_ref, o_ref):
    c = c_ref[...]
    a = (c * _sigmoid(c)).astype(BF16)
    o_ref[0] = jnp.dot(a, w_ref[0].astype(BF16), preferred_element_type=F32) + b_ref[0]


def _modulation(cvec, w_mod, b_mod):
    n_layers, d, d6 = w_mod.shape
    tn = 1024
    return pl.pallas_call(
        _mod_kernel,
        grid=(n_layers, d6 // tn),
        in_specs=[
            pl.BlockSpec((MOD_ROWS, d), lambda l, j: (0, 0)),
            pl.BlockSpec((1, d, tn), lambda l, j: (l, 0, j)),
            pl.BlockSpec((1, 1, tn), lambda l, j: (l, 0, j)),
        ],
        out_specs=pl.BlockSpec((1, MOD_ROWS, tn), lambda l, j: (l, 0, j)),
        out_shape=jax.ShapeDtypeStruct((n_layers, MOD_ROWS, d6), F32),
        compiler_params=_params(2),
        name="modulation",
    )(cvec, w_mod, b_mod.reshape(n_layers, 1, d6))


def _inproj_kernel(*refs, latent):
    it = iter(refs)
    x_ref, mod_ref, n1_ref = next(it), next(it), next(it)
    if latent:
        cos_ref, sa_ref, sb_ref = next(it), next(it), next(it)
    w_diff, w_ret, w_mla, w_gate, w_qb, w_kn, w_v = (next(it) for _ in range(7))
    dqn_ref, dkn_ref, qag_ref, kvag_ref, mqn_ref, mkn_ref = (next(it) for _ in range(6))
    (dq_ref, dk_ref, dv_ref, rq_ref, rk_ref, rv_ref, rg_ref,
     mq_ref, mk_ref, mv_ref, gate_ref) = (next(it) for _ in range(11))
    if not latent:
        odk_ref, odv_ref, ockv_ref, okr_ref = (next(it) for _ in range(4))

    x = x_ref[0]
    mod = mod_ref[0, 0]
    h = _rms(x, D_MODEL) * n1_ref[...] * (1.0 + mod[1:2]) + mod[0:1]
    hb = h.astype(BF16)

    if latent:
        cos, sa, sb = cos_ref[...], sa_ref[...], sb_ref[...]

        def rope(y):
            return (y * cos + pltpu.roll(y, LANES - 16, 1) * sa + pltpu.roll(y, 16, 1) * sb)
    else:
        def rope(y):
            return y

    def group(z, g):
        return z[:, g * LANES:(g + 1) * LANES]

    r_i = lax.broadcasted_iota(jnp.int32, (LANES, LANES), 0) // DIFF_DH
    c_i = lax.broadcasted_iota(jnp.int32, (LANES, LANES), 1) // DIFF_DH
    seg = jnp.where(r_i == c_i, 1.0, 0.0).astype(BF16)

    def qk_norm(y, gain):
        ss = jnp.dot((y * y).astype(BF16), seg, preferred_element_type=F32)
        return y * lax.rsqrt(ss * (1.0 / DIFF_DH) + EPS) * gain

    zd = jnp.dot(hb, w_diff[...], preferred_element_type=F32)
    n_g = BRANCH_W // LANES
    q_gain = dqn_ref[...] * (DIFF_DH ** -0.5 * LOG2E)
    for g in range(n_g):
        sl = slice(g * LANES, (g + 1) * LANES)
        dq_ref[0, :, sl] = rope(qk_norm(group(zd, g), q_gain)).astype(BF16)
        kn = qk_norm(group(zd, n_g + g), dkn_ref[...])
        dk_ref[0, :, sl] = rope(kn).astype(BF16)
        v = group(zd, 2 * n_g + g)
        dv_ref[0, :, sl] = v.astype(BF16)
        if not latent:
            odk_ref[0, :, sl] = kn
            odv_ref[0, :, sl] = v

    zr = jnp.dot(hb, w_ret[...], preferred_element_type=F32)
    for g in range(n_g):
        sl = slice(g * LANES, (g + 1) * LANES)
        rq_ref[0, :, sl] = rope(group(zr, g)).astype(BF16)
        rk_ref[0, :, sl] = (rope(group(zr, n_g + g)) * RET_DK ** -0.5).astype(BF16)
        rv_ref[0, :, sl] = group(zr, 2 * n_g + g).astype(BF16)
        rg_ref[0, :, sl] = group(zr, 3 * n_g + g).astype(BF16)

    zm = jnp.dot(hb, w_mla[...], preferred_element_type=F32)
    qa = _rms(zm[:, :MLA_Q_LORA], MLA_Q_LORA) * qag_ref[...]
    mqz = jnp.dot(qa.astype(BF16), w_qb[...], preferred_element_type=F32)
    ckv = _rms(zm[:, MLA_Q_LORA:MLA_Q_LORA + MLA_KV_LORA], MLA_KV_LORA) * kvag_ref[...]
    ckvb = ckv.astype(BF16)
    k_nope = jnp.dot(ckvb, w_kn[...], preferred_element_type=F32)
    mv_ref[0] = jnp.dot(ckvb, w_v[...], preferred_element_type=F32).astype(BF16)
    kr = zm[:, MLA_Q_LORA + MLA_KV_LORA:]
    ss_kr = jnp.sum(kr * kr, axis=-1, keepdims=True)
    if not latent:
        ockv_ref[0] = ckv
        okr_ref[0] = kr[:, :MLA_ROPE]
    q_gain = mqn_ref[...] * (MLA_QK ** -0.5 * LOG2E)
    k_gain = mkn_ref[...]
    for hd in range(MLA_HEADS):
        lo = slice(hd * MLA_PAD, hd * MLA_PAD + LANES)
        hi = slice(hd * MLA_PAD + LANES, (hd + 1) * MLA_PAD)
        qh = _rms(mqz[:, hd * MLA_PAD:(hd + 1) * MLA_PAD], MLA_QK) * q_gain
        mq_ref[0, :, lo] = qh[:, :LANES].astype(BF16)
        mq_ref[0, :, hi] = rope(qh[:, LANES:]).astype(BF16)
        kno = k_nope[:, hd * LANES:(hd + 1) * LANES]
        ss = jnp.sum(kno * kno, axis=-1, keepdims=True) + ss_kr
        rstd = lax.rsqrt(ss * (1.0 / MLA_QK) + EPS)
        mk_ref[0, :, lo] = (kno * rstd * k_gain[:, :LANES]).astype(BF16)
        mk_ref[0, :, hi] = rope(kr * rstd * k_gain[:, LANES:]).astype(BF16)

    gate_ref[0] = jnp.dot(hb, w_gate[...], preferred_element_type=F32).astype(BF16)


def _inproj(x, mod, layer, lw, rope_tabs, tm):
    b_sz, n, d = x.shape
    latent = rope_tabs is not None
    tm = min(tm, n)
    grid = (b_sz, n // tm)

    def tok(width):
        return pl.BlockSpec((1, tm, width), lambda b, i: (b, i, 0))

    if latent:
        mod_spec = pl.BlockSpec((1, 1, 6, d), lambda b, i: (layer, 1 + b, 0, 0))
    else:
        mod_spec = pl.BlockSpec((1, 1, 6, d), lambda b, i: (layer, 0, 0, 0))
    in_specs = [tok(d), mod_spec, _resident((1, d))]
    args = [x, mod, lw["norm1"]]
    if latent:
        in_specs += [pl.BlockSpec((tm, LANES), lambda b, i: (i, 0))] * 3
        args += list(rope_tabs)
    weights = [lw["w_diff"], lw["w_ret"], lw["w_mla"], lw["w_gate"], lw["w_qb"], lw["w_kn"], lw["w_v"]]
    gains = [lw["dqn"], lw["dkn"], lw["qa_g"], lw["kva_g"], lw["mqn"], lw["mkn"]]
    for a in weights + gains:
        in_specs.append(_resident(a.shape))
        args.append(a)

    widths = [BRANCH_W] * 7 + [MLA_HEADS * MLA_PAD] * 2 + [MLA_HEADS * MLA_V, N_BRANCH * d]
    out_specs = [tok(w) for w in widths]
    out_shape = [jax.ShapeDtypeStruct((b_sz, n, w), BF16) for w in widths]
    if not latent:
        for w in (BRANCH_W, BRANCH_W, MLA_KV_LORA, MLA_ROPE):
            out_specs.append(tok(w))
            out_shape.append(jax.ShapeDtypeStruct((b_sz, n, w), F32))
    return pl.pallas_call(
        functools.partial(_inproj_kernel, latent=latent),
        grid=grid, in_specs=in_specs, out_specs=out_specs, out_shape=out_shape,
        compiler_params=_params(2),
        name="inproj_latent" if latent else "inproj_context",
    )(*args)


def _attn_kernel(*refs, diff, has_cache, n_new, n_past, tq, ck, hpb, lam_init):
    it = iter(refs)
    q_ref, kn_ref, vn_ref = next(it), next(it), next(it)
    if has_cache:
        kc_ref, vc_ref = next(it), next(it)
    if diff:
        lmb_ref, sub_ref = next(it), next(it)
    o_ref = next(it)
    s_bufs = (next(it), next(it))
    q_tiles = n_new // tq
    n_tiles = hpb * q_tiles
    dq = q_ref.shape[-1] // hpb

    def chunks_of(head):
        kl = slice(head * dq, (head + 1) * dq)
        vl = slice(head * HEAD_W, (head + 1) * HEAD_W)
        out = [(lambda j=j: kn_ref[0, j * ck:(j + 1) * ck, kl], lambda j=j: vn_ref[0, j * ck:(j + 1) * ck, vl])
               for j in range(n_new // ck)]
        if has_cache:
            out += [(lambda j=j: kc_ref[0, 0, j * ck:(j + 1) * ck, kl].astype(BF16),
                     lambda j=j: vc_ref[0, 0, j * ck:(j + 1) * ck, vl].astype(BF16))
                    for j in range(n_past // ck)]
        return out

    def place(t):
        if hpb == 1:
            return 0, pl.ds(pl.multiple_of(t * tq, tq), tq)
        return t // q_tiles, slice((t % q_tiles) * tq, (t % q_tiles + 1) * tq)

    rows = 2 * tq if diff else tq

    def cols(j):
        return slice(j * ck, (j + 1) * ck)


    def score_stage(t, slot, _):
        head, q_rows = place(t)
        q = q_ref[0, q_rows, head * dq:(head + 1) * dq]
        if diff:
            lane = lax.broadcasted_iota(jnp.int32, q.shape, 1)
            qf = q.astype(F32)
            q = jnp.concatenate([jnp.where(lane < DIFF_DH, qf, 0.0), jnp.where(lane >= DIFF_DH, qf, 0.0)],
                                axis=0).astype(BF16)
        m_acc = jnp.full((rows, LANES), -jnp.inf, F32)
        for j, (k_of, _) in enumerate(chunks_of(head)):
            s = lax.dot_general(q, k_of(), (((1,), (1,)), ((), ())), preferred_element_type=F32)
            s_bufs[slot][:, cols(j)] = s
            for u in range(ck // LANES):
                m_acc = jnp.maximum(m_acc, s[:, u * LANES:(u + 1) * LANES])
            yield
        return jnp.broadcast_to(jnp.max(m_acc, axis=-1, keepdims=True), (rows, LANES))

    def slab(j, u):
        return slice(j * ck + u * LANES, j * ck + (u + 1) * LANES)

    def exp_of(slot, j, m):
        return [jnp.exp2(s_bufs[slot][:, slab(j, u)] - m) for u in range(ck // LANES)]

    def softmax_value_stage(t, slot, m):
        head, q_rows = place(t)
        acc = jnp.zeros((rows, HEAD_W + LANES), F32)
        ones = jnp.ones((ck, LANES), BF16)
        for j, (_, v_of) in enumerate(chunks_of(head)):
            e = jnp.concatenate([x.astype(BF16) for x in exp_of(slot, j, m)], axis=1)
            acc = acc + jnp.dot(e, jnp.concatenate([v_of(), ones], axis=1), preferred_element_type=F32)
            yield
        o = acc[:, :HEAD_W] / acc[:, HEAD_W:]
        if diff:
            lmb = lmb_ref[...]
            lam = (jnp.exp(jnp.sum(lmb[0:1] * lmb[1:2], axis=-1, keepdims=True))
                   - jnp.exp(jnp.sum(lmb[2:3] * lmb[3:4], axis=-1, keepdims=True)) + lam_init)
            o = o[:tq] - lam * o[tq:]
            o = _rms(o, HEAD_W) * sub_ref[...] * (1.0 - lam_init)
        o_ref[0, q_rows, head * HEAD_W:(head + 1) * HEAD_W] = o.astype(BF16)

    stage_fns = (score_stage, softmax_value_stage)
    n_stage = len(stage_fns)
    n_slot = 2

    def step(u, u_static, inputs):
        gens = [(k, fn(u - k, (u_static - k) % n_slot, inputs[k]))
                for k, fn in enumerate(stage_fns) if inputs[k] is not _IDLE]
        new = {}
        while gens:
            for item in list(gens):
                try:
                    next(item[1])
                except StopIteration as done:
                    new[item[0]] = done.value
                    gens.remove(item)
        return new

    def inputs_of(results, u):
        return [(results[k - 1] if k else None) if 0 <= u - k < n_tiles else _IDLE for k in range(n_stage)]

    n_steps = n_tiles + n_stage - 1
    steady = [u for u in range(n_steps) if _IDLE not in inputs_of([None] * n_stage, u)]
    n_iter = len(steady) // n_slot
    rolled = set(steady[:n_iter * n_slot]) if n_iter >= 2 else set()
    assert not rolled or hpb == 1

    results = [None] * n_stage
    u = 0
    while u < n_steps:
        if u in rolled:
            def body(i, res, u0=u):
                res = list(res) + [None]
                for r in range(n_slot):
                    new = step(u0 + i * n_slot + r, u0 + r, inputs_of(res, u0 + r))
                    res = [new[k] for k in range(n_stage)]
                return tuple(res[:-1])

            results = list(lax.fori_loop(0, n_iter, body, tuple(results[:-1]))) + [None]
            u += n_iter * n_slot
        else:
            new = step(u, u, inputs_of(results, u))
            results = [new.get(k) for k in range(n_stage)]
            u += 1


def _attention(q, k_new, v_new, cache, layer, *, diff, lam_init=0.0, lmb=None, subln=None, tq=256, ck=512):
    b_sz, n, _ = q.shape
    dq = LANES if diff else MLA_PAD
    n_heads = DIFF_HEADS if diff else MLA_HEADS
    tq = min(tq, n)
    has_cache = cache is not None
    n_past = cache[0].shape[2] if has_cache else 0
    ck = min(ck, n, n_past) if has_cache else min(ck, n)
    hpb = n_heads if n // tq == 1 else 1

    in_specs = [
        pl.BlockSpec((1, n, hpb * dq), lambda b, h: (b, 0, h)),
        pl.BlockSpec((1, n, hpb * dq), lambda b, h: (b, 0, h)),
        pl.BlockSpec((1, n, hpb * HEAD_W), lambda b, h: (b, 0, h)),
    ]
    args = [q, k_new, v_new]
    if has_cache:
        in_specs += [
            pl.BlockSpec((1, 1, n_past, hpb * dq), lambda b, h: (b, layer, 0, h)),
            pl.BlockSpec((1, 1, n_past, hpb * HEAD_W), lambda b, h: (b, layer, 0, h)),
        ]
        args += list(cache)
    if diff:
        in_specs += [_resident(lmb.shape), _resident(subln.shape)]
        args += [lmb, subln]
    rows = 2 * tq if diff else tq
    return pl.pallas_call(
        functools.partial(_attn_kernel, diff=diff, has_cache=has_cache, n_new=n, n_past=n_past,
                          tq=tq, ck=ck, hpb=hpb, lam_init=lam_init),
        grid=(b_sz, n_heads // hpb), in_specs=in_specs,
        out_specs=pl.BlockSpec((1, n, hpb * HEAD_W), lambda b, h: (b, 0, h)),
        out_shape=jax.ShapeDtypeStruct((b_sz, n, n_heads * HEAD_W), BF16),
        scratch_shapes=[pltpu.VMEM((rows, n + n_past), F32)] * 2,
        compiler_params=_params(2),
        name=("diff_attn" if diff else "mla_attn") + ("_latent" if has_cache else "_context"),
    )(*args)


def _log_sigmoid(x):
    return jnp.minimum(x, 0.0) - jnp.log(1.0 + jnp.exp(-jnp.abs(x)))


def _ret_kernel(*refs, latent, n_chunks):
    it = iter(refs)
    dec_ref, q_ref, k_ref, v_ref, g_ref, gn_ref = (next(it) for _ in range(6))
    if latent:
        r0_ref = next(it)
    o_ref = next(it)
    if not latent:
        st_ref = next(it)
    rcat_scr, u_scr = next(it), next(it)

    c_len = RET_CHUNK
    hd = pl.program_id(1)
    lg_f = _log_sigmoid(jnp.full((1, 1), dec_ref[0, hd], F32))
    lg_b = _log_sigmoid(jnp.full((1, 1), dec_ref[1, hd], F32))
    row = lax.broadcasted_iota(jnp.int32, (c_len, c_len), 0).astype(F32)
    col = lax.broadcasted_iota(jnp.int32, (c_len, c_len), 1).astype(F32)
    lane_fwd = lax.broadcasted_iota(jnp.int32, (c_len, 2 * RET_DK), 1) < RET_DK
    d_in = 0.5 * jnp.where(row >= col, jnp.exp((row - col) * lg_f), jnp.exp((col - row) * lg_b))
    q_dec = jnp.where(lane_fwd, jnp.exp((row + 1.0) * lg_f), jnp.exp((c_len - row) * lg_b))
    k_dec = jnp.where(lane_fwd, jnp.exp((c_len - 1.0 - row) * lg_f), jnp.exp(row * lg_b))
    dec_f = jnp.exp(c_len * lg_f)
    dec_b = jnp.exp(c_len * lg_b)

    def chunk(c):
        return pl.ds(pl.multiple_of(c * c_len, c_len), c_len)

    def inc_body(c, carry):
        kd = (k_ref[0, chunk(c), :].astype(F32) * k_dec).astype(BF16)
        u_scr[c] = lax.dot_general(kd, v_ref[0, chunk(c), :], (((0,), (0,)), ((), ())),
                                   preferred_element_type=F32)
        return carry

    unroll = min(16, n_chunks)
    lax.fori_loop(0, n_chunks, inc_body, 0, unroll=unroll)

    if latent:
        r_f0 = r0_ref[0, 0, 0, 0]
        r_b0 = r0_ref[0, 0, 1, 0]
    else:
        r_f0 = jnp.zeros((RET_DK, RET_DV), F32)
        r_b0 = r_f0

    def fwd_body(c, r):
        rcat_scr[c, 0:RET_DK, :] = r.astype(BF16)
        return dec_f * r + u_scr[c, 0:RET_DK, :]

    def bwd_body(t, r):
        c = n_chunks - 1 - t
        rcat_scr[c, RET_DK:2 * RET_DK, :] = r.astype(BF16)
        return dec_b * r + u_scr[c, RET_DK:2 * RET_DK, :]

    r_f = lax.fori_loop(0, n_chunks, fwd_body, r_f0)
    r_b = lax.fori_loop(0, n_chunks, bwd_body, r_b0)
    if not latent:
        st_ref[0, 0, 0] = r_f
        st_ref[0, 1, 0] = r_b

    gn = gn_ref[...]

    def out_body(c, carry):
        qc = q_ref[0, chunk(c), :]
        kc = k_ref[0, chunk(c), :]
        vc = v_ref[0, chunk(c), :]
        a = lax.dot_general(qc, kc, (((1,), (1,)), ((), ())), preferred_element_type=F32) * d_in
        qd = (qc.astype(F32) * q_dec).astype(BF16)
        o = (jnp.dot(a.astype(BF16), vc, preferred_element_type=F32)
             + jnp.dot(qd, rcat_scr[c], preferred_element_type=F32))
        gate = g_ref[0, chunk(c), :].astype(F32)
        y = _rms(o, RET_DV) * gn * (gate * _sigmoid(gate))
        o_ref[0, chunk(c), :] = y.astype(BF16)
        return carry

    lax.fori_loop(0, n_chunks, out_body, 0, unroll=unroll)


def _retention(rq, rk, rv, rg, ret_decay, ret_gn, state, layer):
    b_sz, n, _ = rq.shape
    latent = state is not None
    n_chunks = n // RET_CHUNK
    tokw = pl.BlockSpec((1, n, HEAD_W), lambda b, h: (b, 0, h))
    in_specs = [pl.BlockSpec(memory_space=pltpu.SMEM), tokw, tokw, tokw, tokw, _resident(ret_gn.shape)]
    args = [ret_decay, rq, rk, rv, rg, ret_gn]
    if latent:
        in_specs.append(pl.BlockSpec((1, 1, 2, 1, RET_DK, RET_DV), lambda b, h: (b, layer, 0, h, 0, 0)))
        args.append(state)
    out_specs = [tokw]
    out_shape = [jax.ShapeDtypeStruct((b_sz, n, RET_HEADS * HEAD_W), BF16)]
    if not latent:
        out_specs.append(pl.BlockSpec((1, 2, 1, RET_DK, RET_DV), lambda b, h: (b, 0, h, 0, 0)))
        out_shape.append(jax.ShapeDtypeStruct((b_sz, 2, RET_HEADS, RET_DK, RET_DV), F32))
    return pl.pallas_call(
        functools.partial(_ret_kernel, latent=latent, n_chunks=n_chunks),
        grid=(b_sz, RET_HEADS), in_specs=in_specs, out_specs=out_specs, out_shape=out_shape,
        scratch_shapes=[pltpu.VMEM((n_chunks, RET_CHUNK, RET_DV), BF16),
                        pltpu.VMEM((n_chunks, RET_CHUNK, RET_DV), F32)],
        compiler_params=_params(2),
        name="retention_latent" if latent else "retention_context",
    )(*args)


def _ctxkeys_kernel(ckv_ref, kr_ref, w_kn, w_v, mkn_ref, mk_ref, mv_ref):
    ckvb = ckv_ref[0, 0].astype(BF16)
    kr = kr_ref[0, 0]
    k_nope = jnp.dot(ckvb, w_kn[...], preferred_element_type=F32)
    mv_ref[0, 0] = jnp.dot(ckvb, w_v[...], preferred_element_type=F32).astype(BF16)
    ss_kr = jnp.sum(kr * kr, axis=-1, keepdims=True)
    k_gain = mkn_ref[...]
    for hd in range(MLA_HEADS):
        kno = k_nope[:, hd * LANES:(hd + 1) * LANES]
        ss = jnp.sum(kno * kno, axis=-1, keepdims=True) + ss_kr
        rstd = lax.rsqrt(ss * (1.0 / MLA_QK) + EPS)
        mk_ref[0, 0, :, hd * MLA_PAD:hd * MLA_PAD + LANES] = (kno * rstd * k_gain[:, :LANES]).astype(BF16)
        mk_ref[0, 0, :, hd * MLA_PAD + LANES:(hd + 1) * MLA_PAD] = (kr * rstd * k_gain[:, LANES:]).astype(BF16)


def _context_keys(cache_ckv, cache_kr_pad, lws):
    b_sz, n_layers, p, _ = cache_ckv.shape
    outs = []
    for layer, lw in enumerate(lws):
        outs.append(pl.pallas_call(
            _ctxkeys_kernel,
            grid=(b_sz,),
            in_specs=[
                pl.BlockSpec((1, 1, p, MLA_KV_LORA), lambda b, layer=layer: (b, layer, 0, 0)),
                pl.BlockSpec((1, 1, p, LANES), lambda b, layer=layer: (b, layer, 0, 0)),
                _resident(lw["w_kn"].shape), _resident(lw["w_v"].shape), _resident(lw["mkn"].shape),
            ],
            out_specs=[pl.BlockSpec((1, 1, p, MLA_HEADS * MLA_PAD), lambda b: (b, 0, 0, 0)),
                       pl.BlockSpec((1, 1, p, MLA_HEADS * MLA_V), lambda b: (b, 0, 0, 0))],
            out_shape=[jax.ShapeDtypeStruct((b_sz, 1, p, MLA_HEADS * MLA_PAD), BF16),
                       jax.ShapeDtypeStruct((b_sz, 1, p, MLA_HEADS * MLA_V), BF16)],
            compiler_params=_params(1),
            name="mla_context_keys",
        )(cache_ckv, cache_kr_pad, lw["w_kn"], lw["w_v"], lw["mkn"]))
    return outs


def _post_kernel(x_ref, oa_ref, or_ref, om_ref, g_ref, mod_ref, n2_ref, wb_ref, wo_ref, wu_ref, wd_ref,
                 o_ref, *, ff_chunk):
    x = x_ref[0]
    mod = mod_ref[0, 0]
    d = x.shape[-1]
    merged = jnp.zeros(x.shape, F32)
    for i, br_ref in enumerate((oa_ref, or_ref, om_ref)):
        gate = _sigmoid(g_ref[0, :, i * d:(i + 1) * d].astype(F32))
        merged = merged + gate * jnp.dot(br_ref[0], wb_ref[i], preferred_element_type=F32)
    mix = jnp.dot(merged.astype(BF16), wo_ref[...], preferred_element_type=F32)
    x1 = x + mod[2:3] * mix
    h2 = (_rms(x1, d) * n2_ref[...] * (1.0 + mod[4:5]) + mod[3:4]).astype(BF16)
    acc = jnp.zeros(x.shape, F32)
    for c in range(D_FF // ff_chunk):
        u = jnp.dot(h2, wu_ref[:, c * ff_chunk:(c + 1) * ff_chunk], preferred_element_type=F32)
        u = jnp.square(jnp.maximum(u, 0.0)).astype(BF16)
        acc = acc + jnp.dot(u, wd_ref[c * ff_chunk:(c + 1) * ff_chunk, :], preferred_element_type=F32)
    o_ref[0] = x1 + mod[5:6] * acc


def _post(x, oa, orr, om, gates, mod, layer, lw, latent, tm):
    b_sz, n, d = x.shape
    tm = min(tm, n)

    def tok(width):
        return pl.BlockSpec((1, tm, width), lambda b, i: (b, i, 0))

    if latent:
        mod_spec = pl.BlockSpec((1, 1, 6, d), lambda b, i: (layer, 1 + b, 0, 0))
    else:
        mod_spec = pl.BlockSpec((1, 1, 6, d), lambda b, i: (layer, 0, 0, 0))
    weights = [lw["norm2"], lw["w_branch"], lw["w_out"], lw["w_up"], lw["w_down"]]
    return pl.pallas_call(
        functools.partial(_post_kernel, ff_chunk=1024),
        grid=(b_sz, n // tm),
        in_specs=[tok(d), tok(BRANCH_W), tok(BRANCH_W), tok(BRANCH_W), tok(N_BRANCH * d), mod_spec]
                 + [_resident(w.shape) for w in weights],
        out_specs=tok(d),
        out_shape=jax.ShapeDtypeStruct((b_sz, n, d), F32),
        compiler_params=_params(2),
        name="merge_mlp_latent" if latent else "merge_mlp_context",
    )(x, oa, orr, om, gates, mod, *weights)


def _rope_tables(n):
    pos = jnp.arange(n, dtype=jnp.int32)
    row = (pos // GRID_W).astype(F32)
    col = (pos % GRID_W).astype(F32)
    axis_dim = ROT_DIM // 2
    inv = ROPE_BASE ** (-jnp.arange(0, axis_dim, 2, dtype=F32) / axis_dim)
    ang_r = row[:, None] * inv[None, :]
    ang_c = col[:, None] * inv[None, :]
    zeros = jnp.zeros_like(ang_r)
    cos = jnp.concatenate([jnp.cos(ang_r)] * 2 + [jnp.cos(ang_c)] * 2, axis=-1)
    sin_a = jnp.concatenate([-jnp.sin(ang_r), zeros, -jnp.sin(ang_c), zeros], axis=-1)
    sin_b = jnp.concatenate([zeros, jnp.sin(ang_r), zeros, jnp.sin(ang_c)], axis=-1)
    reps = LANES // ROT_DIM
    return tuple(jnp.tile(t, (1, reps)) for t in (cos, sin_a, sin_b))


def _layer_weights(l, w_in, norm1, norm2, diff_qn, diff_kn, diff_lambda, diff_subln, ret_decay, ret_gn,
                   mla_qa_norm, w_mla_qb, mla_kva_norm, w_mla_kvb, mla_qn, mla_kn, w_branch, w_out, w_up,
                   w_down):
    d = w_in.shape[1]
    w = w_in[l]
    o_ret = 3 * BRANCH_W
    o_rk = o_ret + RET_HEADS * RET_DK
    o_rv = o_rk + RET_HEADS * RET_DK
    o_rg = o_rv + RET_HEADS * RET_DV
    o_qa = o_rg + RET_HEADS * RET_DV
    o_kva = o_qa + MLA_Q_LORA
    o_kr = o_kva + MLA_KV_LORA
    o_gate = o_kr + MLA_ROPE

    def dup_heads(cols):
        c = cols.reshape(d, RET_HEADS, RET_DK)
        return jnp.concatenate([c, c], axis=-1).reshape(d, RET_HEADS * 2 * RET_DK)

    w_ret = jnp.concatenate([dup_heads(w[:, o_ret:o_rk]), dup_heads(w[:, o_rk:o_rv]), w[:, o_rv:o_qa]], axis=-1)
    w_mla = jnp.concatenate([w[:, o_qa:o_gate], jnp.zeros((d, LANES - MLA_ROPE), w.dtype)], axis=-1)
    w_qb = jnp.pad(w_mla_qb[l].reshape(MLA_Q_LORA, MLA_HEADS, MLA_QK), ((0, 0), (0, 0), (0, MLA_PAD - MLA_QK)))
    w_kvb = w_mla_kvb[l].reshape(MLA_KV_LORA, MLA_HEADS, MLA_NOPE + MLA_V)

    def pad_gain(g):
        return jnp.pad(g, (0, MLA_PAD - MLA_QK))[None, :]

    return {
        "w_diff": w[:, :o_ret].astype(BF16),
        "w_ret": w_ret.astype(BF16),
        "w_mla": w_mla.astype(BF16),
        "w_gate": w[:, o_gate:].astype(BF16),
        "w_qb": w_qb.reshape(MLA_Q_LORA, MLA_HEADS * MLA_PAD).astype(BF16),
        "w_kn": w_kvb[:, :, :MLA_NOPE].reshape(MLA_KV_LORA, MLA_HEADS * MLA_NOPE).astype(BF16),
        "w_v": w_kvb[:, :, MLA_NOPE:].reshape(MLA_KV_LORA, MLA_HEADS * MLA_V).astype(BF16),
        "norm1": norm1[l][None, :], "norm2": norm2[l][None, :],
        "dqn": jnp.tile(diff_qn[l], LANES // DIFF_DH)[None, :],
        "dkn": jnp.tile(diff_kn[l], LANES // DIFF_DH)[None, :],
        "qa_g": mla_qa_norm[l][None, :], "kva_g": mla_kva_norm[l][None, :],
        "mqn": pad_gain(mla_qn[l]), "mkn": pad_gain(mla_kn[l]),
        "lmb": diff_lambda[l], "subln": diff_subln[l][None, :],
        "ret_decay": ret_decay[l], "ret_gn": ret_gn[l][None, :],
        "w_branch": w_branch[l].astype(BF16), "w_out": w_out[l].astype(BF16),
        "w_up": w_up[l].astype(BF16), "w_down": w_down[l].astype(BF16),
    }


def _lambda_init(l):
    a, b, c = LAMBDA_INIT_BASE
    return a - b * math.exp(-c * l)


def _block(x, mod, layer, lw, rope_tabs, caches, tm_in, tm_post, tq):
    latent = rope_tabs is not None
    b_sz, n, d = x.shape

    def flat(a):
        return a if latent else a.reshape(1, b_sz * n, a.shape[-1])

    outs = [o.reshape(b_sz, n, o.shape[-1]) for o in _inproj(flat(x), mod, layer, lw, rope_tabs, tm_in)]
    dq, dk, dv, rq, rk, rv, rg, mq, mk, mv, gates = outs[:11]
    lam_init = _lambda_init(layer)
    diff_cache = caches["diff"] if latent else None
    mla_cache = caches["mla"][layer] if latent else None
    oa = _attention(dq, dk, dv, diff_cache, layer, diff=True, lam_init=lam_init, lmb=lw["lmb"],
                    subln=lw["subln"], tq=tq)
    om = _attention(mq, mk, mv, mla_cache, 0, diff=False, tq=4 * tq)
    ret = _retention(rq, rk, rv, rg, lw["ret_decay"], lw["ret_gn"], caches["state"] if latent else None, layer)
    x = _post(flat(x), flat(oa), flat(ret[0]), flat(om), flat(gates), mod, layer, lw, latent,
              tm_post).reshape(b_sz, n, d)
    if latent:
        return x, None
    return x, (outs[11], outs[12], outs[13], outs[14], ret[1])


def kernel(x_prompt, x_sample, cache_diff_k, cache_diff_v, cache_mla_ckv, cache_mla_krope, state_ret,
           c, c_ctx, w_mod, b_mod, norm1, norm2, w_in, diff_qn, diff_kn, diff_lambda, diff_subln,
           ret_decay, ret_gn, mla_qa_norm, w_mla_qb, mla_kva_norm, w_mla_kvb, mla_qn, mla_kn,
           w_branch, w_out, w_up, w_down):
    n_layers = w_in.shape[0]
    d = x_prompt.shape[-1]
    bd, n_lat = x_sample.shape[:2]
    b_ctx, n_ctx = x_prompt.shape[:2]
    p = cache_diff_k.shape[2]
    assert bd + 1 <= MOD_ROWS

    lws = [_layer_weights(l, w_in, norm1, norm2, diff_qn, diff_kn, diff_lambda, diff_subln, ret_decay, ret_gn,
                          mla_qa_norm, w_mla_qb, mla_kva_norm, w_mla_kvb, mla_qn, mla_kn, w_branch, w_out,
                          w_up, w_down) for l in range(n_layers)]

    cvec = jnp.concatenate([c_ctx[None, :], c, jnp.zeros((MOD_ROWS - 1 - bd, d), F32)], axis=0)
    mod = _modulation(cvec, w_mod, b_mod).reshape(n_layers, MOD_ROWS, 6, d)

    xp = x_prompt
    ctx_outs = []
    for l in range(n_layers):
        xp, ctx = _block(xp, mod, l, lws[l], None, None, 512, 512, 256)
        ctx_outs.append(ctx)
    new_diff_k = jnp.stack([o[0] for o in ctx_outs], axis=1).reshape(b_ctx, n_layers, n_ctx, 2 * DIFF_HEADS, DIFF_DH)
    new_diff_v = jnp.stack([o[1] for o in ctx_outs], axis=1).reshape(b_ctx, n_layers, n_ctx, DIFF_HEADS, 2 * DIFF_DH)
    new_mla_ckv = jnp.stack([o[2] for o in ctx_outs], axis=1)
    new_mla_krope = jnp.stack([o[3] for o in ctx_outs], axis=1)
    new_state_ret = jnp.stack([o[4] for o in ctx_outs], axis=1)

    caches = {
        "diff": (cache_diff_k.reshape(bd, n_layers, p, 2 * DIFF_HEADS * DIFF_DH),
                 cache_diff_v.reshape(bd, n_layers, p, DIFF_HEADS * 2 * DIFF_DH)),
        "mla": _context_keys(cache_mla_ckv, jnp.pad(cache_mla_krope, ((0, 0),) * 3 + ((0, LANES - MLA_ROPE),)), lws),
        "state": state_ret,
    }
    rope_tabs = _rope_tables(n_lat)
    xs = x_sample
    for l in range(n_layers):
        xs, _ = _block(xs, mod, l, lws[l], rope_tabs, caches, 512, 512, 256)

    return (xp, xs, new_diff_k, new_diff_v, new_mla_ckv, new_mla_krope, new_state_ret)
```

```python
import functools
import math

import jax
import jax.numpy as jnp
from jax import lax
from jax.experimental import pallas as pl
from jax.experimental.pallas import tpu as pltpu

F32 = jnp.float32
BF16 = jnp.bfloat16

D_MODEL = 1024
GRID_W = 64
DIFF_HEADS = 4
DIFF_DH = 64
RET_HEADS = 4
RET_DK = 64
RET_DV = 128
RET_CHUNK = 128
MLA_HEADS = 4
MLA_Q_LORA = 384
MLA_KV_LORA = 256
MLA_NOPE = 128
MLA_ROPE = 64
MLA_V = 128
MLA_QK = MLA_NOPE + MLA_ROPE
N_BRANCH = 3
BRANCH_W = 512
D_FF = 4 * D_MODEL
ROT_DIM = 64
ROPE_BASE = 10000.0
EPS = 1e-6
LAMBDA_INIT_BASE = (0.8, 0.6, 0.3)

LANES = 128
MXU_W = 256
MLA_PAD = 2 * LANES
HEAD_W = 128
LOG2E = 1.4426950408889634
VMEM_LIMIT = 56 * 1024 * 1024
_IDLE = object()
MOD_ROWS = 16


def _resident(shape):
    nd = len(shape)
    return pl.BlockSpec(shape, lambda *_: (0,) * nd, pipeline_mode=pl.Buffered(1))


def _params(n_grid):
    return pltpu.CompilerParams(dimension_semantics=("arbitrary",) * n_grid,
                                vmem_limit_bytes=VMEM_LIMIT)


def _sigmoid(x):
    return 1.0 / (1.0 + jnp.exp(-x))


def _rms(x, width):
    ss = jnp.sum(x * x, axis=-1, keepdims=True)
    return x * lax.rsqrt(ss * (1.0 / width) + EPS)


def _mod_kernel(c_ref, w_ref, b_ref, o_ref):
    c = c_ref[...]
    a = (c * _sigmoid(c)).astype(BF16)
    o_ref[0] = jnp.dot(a, w_ref[0].astype(BF16), preferred_element_type=F32) + b_ref[0]


def _modulation(cvec, w_mod, b_mod):
    n_layers, d, d6 = w_mod.shape
    tn = d6 // 2
    return pl.pallas_call(
        _mod_kernel,
        grid=(n_layers, d6 // tn),
        in_specs=[
            pl.BlockSpec((MOD_ROWS, d), lambda l, j: (0, 0)),
            pl.BlockSpec((1, d, tn), lambda l, j: (l, 0, j)),
            pl.BlockSpec((1, 1, tn), lambda l, j: (l, 0, j)),
        ],
        out_specs=pl.BlockSpec((1, MOD_ROWS, tn), lambda l, j: (l, 0, j)),
        out_shape=jax.ShapeDtypeStruct((n_layers, MOD_ROWS, d6), F32),
        compiler_params=_params(2),
        name="modulation",
    )(cvec, w_mod, b_mod.reshape(n_layers, 1, d6))


def _inproj_kernel(*refs, latent):
    it = iter(refs)
    x_ref, mod_ref, n1_ref = next(it), next(it), next(it)
    if latent:
        cos_ref, sa_ref, sb_ref = next(it), next(it), next(it)
    w_diff, w_ret, w_mla, w_gate, w_qb, w_kn, w_v = (next(it) for _ in range(7))
    dqn_ref, dkn_ref, qag_ref, kvag_ref, mqn_ref, mkn_ref = (next(it) for _ in range(6))
    (dq_ref, dk_ref, dv_ref, rq_ref, rk_ref, rv_ref, rg_ref,
     mq_ref, mk_ref, mv_ref, gate_ref) = (next(it) for _ in range(11))
    if not latent:
        odk_ref, odv_ref, ockv_ref, okr_ref = (next(it) for _ in range(4))

    x = x_ref[0]
    mod = mod_ref[0, 0]
    h = _rms(x, D_MODEL) * n1_ref[...] * (1.0 + mod[1:2]) + mod[0:1]
    hb = h.astype(BF16)

    if latent:
        cos, sa, sb = cos_ref[...], sa_ref[...], sb_ref[...]

        def rope(y):
            return (y * cos + pltpu.roll(y, LANES - 16, 1) * sa + pltpu.roll(y, 16, 1) * sb)
    else:
        def rope(y):
            return y

    def group(z, g):
        return z[:, g * LANES:(g + 1) * LANES]

    r_i = lax.broadcasted_iota(jnp.int32, (MXU_W, MXU_W), 0) // DIFF_DH
    c_i = lax.broadcasted_iota(jnp.int32, (MXU_W, MXU_W), 1) // DIFF_DH
    seg = jnp.where(r_i == c_i, 1.0, 0.0).astype(BF16)

    def qk_norm(y, gain):
        ss = jnp.dot((y * y).astype(BF16), seg, preferred_element_type=F32)
        return y * lax.rsqrt(ss * (1.0 / DIFF_DH) + EPS) * gain

    zd = jnp.dot(hb, w_diff[...], preferred_element_type=F32)
    n_g = BRANCH_W // LANES
    per_tile = MXU_W // LANES
    q_gain = dqn_ref[...] * (DIFF_DH ** -0.5 * LOG2E)
    for g2 in range(BRANCH_W // MXU_W):
        qn = qk_norm(zd[:, g2 * MXU_W:(g2 + 1) * MXU_W], q_gain)
        kn = qk_norm(zd[:, BRANCH_W + g2 * MXU_W:BRANCH_W + (g2 + 1) * MXU_W], dkn_ref[...])
        for g in range(g2 * per_tile, (g2 + 1) * per_tile):
            sl = slice(g * LANES, (g + 1) * LANES)
            dq_ref[0, :, sl] = rope(group(qn, g % per_tile)).astype(BF16)
            dk_ref[0, :, sl] = rope(group(kn, g % per_tile)).astype(BF16)
            v = group(zd, 2 * n_g + g)
            dv_ref[0, :, sl] = v.astype(BF16)
            if not latent:
                odk_ref[0, :, sl] = group(kn, g % per_tile)
                odv_ref[0, :, sl] = v

    zr = jnp.dot(hb, w_ret[...], preferred_element_type=F32)
    low = lax.broadcasted_iota(jnp.int32, (x.shape[0], LANES), 1) < RET_DK

    def twice(y):
        swapped = pltpu.roll(y, RET_DK, 1)
        return jnp.where(low, y, swapped), jnp.where(low, swapped, y)

    n_p = RET_HEADS * RET_DK // LANES
    for p in range(n_p):
        q_pair = twice(rope(group(zr, p)))
        k_pair = twice(rope(group(zr, n_p + p)) * RET_DK ** -0.5)
        for i in range(2):
            sl = slice((2 * p + i) * LANES, (2 * p + i + 1) * LANES)
            rq_ref[0, :, sl] = q_pair[i].astype(BF16)
            rk_ref[0, :, sl] = k_pair[i].astype(BF16)
    for g in range(n_g):
        sl = slice(g * LANES, (g + 1) * LANES)
        rv_ref[0, :, sl] = group(zr, 2 * n_p + g).astype(BF16)
        rg_ref[0, :, sl] = group(zr, 2 * n_p + n_g + g).astype(BF16)

    zm = jnp.dot(hb, w_mla[...], preferred_element_type=F32)
    qa = _rms(zm[:, :MLA_Q_LORA], MLA_Q_LORA) * qag_ref[...]
    mqz = jnp.dot(qa.astype(BF16), w_qb[...], preferred_element_type=F32)
    ckv = _rms(zm[:, MLA_Q_LORA:MLA_Q_LORA + MLA_KV_LORA], MLA_KV_LORA) * kvag_ref[...]
    ckvb = ckv.astype(BF16)
    k_nope = jnp.dot(ckvb, w_kn[...], preferred_element_type=F32)
    mv_ref[0] = jnp.dot(ckvb, w_v[...], preferred_element_type=F32).astype(BF16)
    kr = zm[:, MLA_Q_LORA + MLA_KV_LORA:]
    ss_kr = jnp.sum(kr * kr, axis=-1, keepdims=True)
    if not latent:
        ockv_ref[0] = ckv
        okr_ref[0] = kr[:, :MLA_ROPE]
    q_gain = mqn_ref[...] * (MLA_QK ** -0.5 * LOG2E)
    k_gain = mkn_ref[...]
    for hd in range(MLA_HEADS):
        lo = slice(hd * MLA_PAD, hd * MLA_PAD + LANES)
        hi = slice(hd * MLA_PAD + LANES, (hd + 1) * MLA_PAD)
        qh = _rms(mqz[:, hd * MLA_PAD:(hd + 1) * MLA_PAD], MLA_QK) * q_gain
        mq_ref[0, :, lo] = qh[:, :LANES].astype(BF16)
        mq_ref[0, :, hi] = rope(qh[:, LANES:]).astype(BF16)
        kno = k_nope[:, hd * LANES:(hd + 1) * LANES]
        ss = jnp.sum(kno * kno, axis=-1, keepdims=True) + ss_kr
        rstd = lax.rsqrt(ss * (1.0 / MLA_QK) + EPS)
        mk_ref[0, :, lo] = (kno * rstd * k_gain[:, :LANES]).astype(BF16)
        mk_ref[0, :, hi] = rope(kr * rstd * k_gain[:, LANES:]).astype(BF16)

    gate_ref[0] = jnp.dot(hb, w_gate[...], preferred_element_type=F32).astype(BF16)


def _inproj(x, mod, layer, lw, rope_tabs, tm):
    b_sz, n, d = x.shape
    latent = rope_tabs is not None
    tm = min(tm, n)
    grid = (b_sz, n // tm)

    def tok(width):
        return pl.BlockSpec((1, tm, width), lambda b, i: (b, i, 0))

    if latent:
        mod_spec = pl.BlockSpec((1, 1, 6, d), lambda b, i: (layer, 1 + b, 0, 0))
    else:
        mod_spec = pl.BlockSpec((1, 1, 6, d), lambda b, i: (layer, 0, 0, 0))
    in_specs = [tok(d), mod_spec, _resident((1, d))]
    args = [x, mod, lw["norm1"]]
    if latent:
        in_specs += [pl.BlockSpec((tm, LANES), lambda b, i: (i, 0))] * 3
        args += list(rope_tabs)
    weights = [lw["w_diff"], lw["w_ret"], lw["w_mla"], lw["w_gate"], lw["w_qb"], lw["w_kn"], lw["w_v"]]
    gains = [lw["dqn"], lw["dkn"], lw["qa_g"], lw["kva_g"], lw["mqn"], lw["mkn"]]
    for a in weights + gains:
        in_specs.append(_resident(a.shape))
        args.append(a)

    widths = [BRANCH_W] * 7 + [MLA_HEADS * MLA_PAD] * 2 + [MLA_HEADS * MLA_V, N_BRANCH * d]
    out_specs = [tok(w) for w in widths]
    out_shape = [jax.ShapeDtypeStruct((b_sz, n, w), BF16) for w in widths]
    if not latent:
        for w in (BRANCH_W, BRANCH_W, MLA_KV_LORA, MLA_ROPE):
            out_specs.append(tok(w))
            out_shape.append(jax.ShapeDtypeStruct((b_sz, n, w), F32))
    return pl.pallas_call(
        functools.partial(_inproj_kernel, latent=latent),
        grid=grid, in_specs=in_specs, out_specs=out_specs, out_shape=out_shape,
        compiler_params=_params(2),
        name="inproj_latent" if latent else "inproj_context",
    )(*args)


def _attn_kernel(*refs, diff, has_cache, n_new, n_past, tq, ck, hpb, lam_init):
    it = iter(refs)
    q_ref, kn_ref, vn_ref = next(it), next(it), next(it)
    if has_cache:
        kc_ref, vc_ref = next(it), next(it)
    if diff:
        lmb_ref, sub_ref = next(it), next(it)
    o_ref = next(it)
    s_bufs = (next(it), next(it))
    q_tiles = n_new // tq
    n_tiles = hpb * q_tiles
    dq = q_ref.shape[-1] // hpb

    def chunks_of(head):
        kl = slice(head * dq, (head + 1) * dq)
        vl = slice(head * HEAD_W, (head + 1) * HEAD_W)
        out = [(lambda j=j: kn_ref[0, j * ck:(j + 1) * ck, kl], lambda j=j: vn_ref[0, j * ck:(j + 1) * ck, vl])
               for j in range(n_new // ck)]
        if has_cache:
            out += [(lambda j=j: kc_ref[0, 0, j * ck:(j + 1) * ck, kl].astype(BF16),
                     lambda j=j: vc_ref[0, 0, j * ck:(j + 1) * ck, vl].astype(BF16))
                    for j in range(n_past // ck)]
        return out

    def place(t):
        if hpb == 1:
            return 0, pl.ds(pl.multiple_of(t * tq, tq), tq)
        return t // q_tiles, slice((t % q_tiles) * tq, (t % q_tiles + 1) * tq)

    rows = 2 * tq if diff else tq

    def cols(j):
        return slice(j * ck, (j + 1) * ck)


    def score_stage(t, slot, _):
        head, q_rows = place(t)
        q = q_ref[0, q_rows, head * dq:(head + 1) * dq]
        if diff:
            lane = lax.broadcasted_iota(jnp.int32, q.shape, 1)
            qf = q.astype(F32)
            q = jnp.concatenate([jnp.where(lane < DIFF_DH, qf, 0.0), jnp.where(lane >= DIFF_DH, qf, 0.0)],
                                axis=0).astype(BF16)
        m_acc = jnp.full((rows, LANES), -jnp.inf, F32)
        for j, (k_of, _) in enumerate(chunks_of(head)):
            s = lax.dot_general(q, k_of(), (((1,), (1,)), ((), ())), preferred_element_type=F32)
            s_bufs[slot][:, cols(j)] = s
            for u in range(ck // LANES):
                m_acc = jnp.maximum(m_acc, s[:, u * LANES:(u + 1) * LANES])
            yield
        return jnp.broadcast_to(jnp.max(m_acc, axis=-1, keepdims=True), (rows, LANES))

    def slab(j, u):
        return slice(j * ck + u * LANES, j * ck + (u + 1) * LANES)

    def exp_of(slot, j, m):
        return [jnp.exp2(s_bufs[slot][:, slab(j, u)] - m) for u in range(ck // LANES)]

    def softmax_value_stage(t, slot, m):
        head, q_rows = place(t)
        acc = jnp.zeros((rows, HEAD_W + LANES), F32)
        ones = jnp.ones((ck, LANES), BF16)
        for j, (_, v_of) in enumerate(chunks_of(head)):
            e = jnp.concatenate([x.astype(BF16) for x in exp_of(slot, j, m)], axis=1)
            acc = acc + jnp.dot(e, jnp.concatenate([v_of(), ones], axis=1), preferred_element_type=F32)
            yield
        o = acc[:, :HEAD_W] / acc[:, HEAD_W:]
        if diff:
            lmb = lmb_ref[...]
            lam = (jnp.exp(jnp.sum(lmb[0:1] * lmb[1:2], axis=-1, keepdims=True))
                   - jnp.exp(jnp.sum(lmb[2:3] * lmb[3:4], axis=-1, keepdims=True)) + lam_init)
            o = o[:tq] - lam * o[tq:]
            o = _rms(o, HEAD_W) * sub_ref[...] * (1.0 - lam_init)
        o_ref[0, q_rows, head * HEAD_W:(head + 1) * HEAD_W] = o.astype(BF16)

    stage_fns = (score_stage, softmax_value_stage)
    n_stage = len(stage_fns)
    n_slot = 2

    def step(u, u_static, inputs):
        gens = [(k, fn(u - k, (u_static - k) % n_slot, inputs[k]))
                for k, fn in enumerate(stage_fns) if inputs[k] is not _IDLE]
        new = {}
        while gens:
            for item in list(gens):
                try:
                    next(item[1])
                except StopIteration as done:
                    new[item[0]] = done.value
                    gens.remove(item)
        return new

    def inputs_of(results, u):
        return [(results[k - 1] if k else None) if 0 <= u - k < n_tiles else _IDLE for k in range(n_stage)]

    n_steps = n_tiles + n_stage - 1
    steady = [u for u in range(n_steps) if _IDLE not in inputs_of([None] * n_stage, u)]
    n_iter = len(steady) // n_slot
    rolled = set(steady[:n_iter * n_slot]) if n_iter >= 2 else set()
    assert not rolled or hpb == 1

    results = [None] * n_stage
    u = 0
    while u < n_steps:
        if u in rolled:
            def body(i, res, u0=u):
                res = list(res) + [None]
                for r in range(n_slot):
                    new = step(u0 + i * n_slot + r, u0 + r, inputs_of(res, u0 + r))
                    res = [new[k] for k in range(n_stage)]
                return tuple(res[:-1])

            results = list(lax.fori_loop(0, n_iter, body, tuple(results[:-1]))) + [None]
            u += n_iter * n_slot
        else:
            new = step(u, u, inputs_of(results, u))
            results = [new.get(k) for k in range(n_stage)]
            u += 1


def _attention(q, k_new, v_new, cache, layer, *, diff, lam_init=0.0, lmb=None, subln=None, tq=256, ck=512):
    b_sz, n, _ = q.shape
    dq = LANES if diff else MLA_PAD
    n_heads = DIFF_HEADS if diff else MLA_HEADS
    tq = min(tq, n)
    has_cache = cache is not None
    n_past = cache[0].shape[2] if has_cache else 0
    ck = min(ck, n, n_past) if has_cache else min(ck, n)
    hpb = n_heads if n // tq == 1 else 1

    in_specs = [
        pl.BlockSpec((1, n, hpb * dq), lambda b, h: (b, 0, h)),
        pl.BlockSpec((1, n, hpb * dq), lambda b, h: (b, 0, h)),
        pl.BlockSpec((1, n, hpb * HEAD_W), lambda b, h: (b, 0, h)),
    ]
    args = [q, k_new, v_new]
    if has_cache:
        in_specs += [
            pl.BlockSpec((1, 1, n_past, hpb * dq), lambda b, h: (b, layer, 0, h)),
            pl.BlockSpec((1, 1, n_past, hpb * HEAD_W), lambda b, h: (b, layer, 0, h)),
        ]
        args += list(cache)
    if diff:
        in_specs += [_resident(lmb.shape), _resident(subln.shape)]
        args += [lmb, subln]
    rows = 2 * tq if diff else tq
    return pl.pallas_call(
        functools.partial(_attn_kernel, diff=diff, has_cache=has_cache, n_new=n, n_past=n_past,
                          tq=tq, ck=ck, hpb=hpb, lam_init=lam_init),
        grid=(b_sz, n_heads // hpb), in_specs=in_specs,
        out_specs=pl.BlockSpec((1, n, hpb * HEAD_W), lambda b, h: (b, 0, h)),
        out_shape=jax.ShapeDtypeStruct((b_sz, n, n_heads * HEAD_W), BF16),
        scratch_shapes=[pltpu.VMEM((rows, n + n_past), F32)] * 2,
        compiler_params=_params(2),
        name=("diff_attn" if diff else "mla_attn") + ("_latent" if has_cache else "_context"),
    )(*args)


def _log_sigmoid(x):
    return jnp.minimum(x, 0.0) - jnp.log(1.0 + jnp.exp(-jnp.abs(x)))


def _ret_kernel(*refs, latent, n_chunks, hpb):
    it = iter(refs)
    dec_ref, q_ref, k_ref, v_ref, g_ref, gn_ref = (next(it) for _ in range(6))
    if latent:
        r0_ref = next(it)
    o_ref = next(it)
    if not latent:
        st_ref = next(it)
    rcat_scr, u_scr = next(it), next(it)

    c_len = RET_CHUNK
    row = lax.broadcasted_iota(jnp.int32, (c_len, c_len), 0).astype(F32)
    col = lax.broadcasted_iota(jnp.int32, (c_len, c_len), 1).astype(F32)
    lane_fwd = lax.broadcasted_iota(jnp.int32, (c_len, 2 * RET_DK), 1) < RET_DK
    gn = gn_ref[...]
    unroll = min(16, n_chunks)

    def chunk(c):
        return pl.ds(pl.multiple_of(c * c_len, c_len), c_len)

    def one_head(hh):
        hd = pl.program_id(1) * hpb + hh
        lanes = slice(hh * HEAD_W, (hh + 1) * HEAD_W)
        lg_f = _log_sigmoid(jnp.full((1, 1), dec_ref[0, hd], F32))
        lg_b = _log_sigmoid(jnp.full((1, 1), dec_ref[1, hd], F32))
        d_in = 0.5 * jnp.where(row >= col, jnp.exp((row - col) * lg_f), jnp.exp((col - row) * lg_b))
        q_dec = jnp.where(lane_fwd, jnp.exp((row + 1.0) * lg_f), jnp.exp((c_len - row) * lg_b))
        k_dec = jnp.where(lane_fwd, jnp.exp((c_len - 1.0 - row) * lg_f), jnp.exp(row * lg_b))
        dec_f = jnp.exp(c_len * lg_f)
        dec_b = jnp.exp(c_len * lg_b)

        def inc_body(c, carry):
            kd = (k_ref[0, chunk(c), lanes].astype(F32) * k_dec).astype(BF16)
            u_scr[hh, c] = lax.dot_general(kd, v_ref[0, chunk(c), lanes], (((0,), (0,)), ((), ())),
                                           preferred_element_type=F32)
            return carry

        lax.fori_loop(0, n_chunks, inc_body, 0, unroll=unroll)

        if latent:
            r_f0 = r0_ref[0, 0, 0, hh]
            r_b0 = r0_ref[0, 0, 1, hh]
        else:
            r_f0 = jnp.zeros((RET_DK, RET_DV), F32)
            r_b0 = r_f0

        def fwd_body(c, r):
            rcat_scr[hh, c, 0:RET_DK, :] = r.astype(BF16)
            return dec_f * r + u_scr[hh, c, 0:RET_DK, :]

        def bwd_body(t, r):
            c = n_chunks - 1 - t
            rcat_scr[hh, c, RET_DK:2 * RET_DK, :] = r.astype(BF16)
            return dec_b * r + u_scr[hh, c, RET_DK:2 * RET_DK, :]

        r_f = lax.fori_loop(0, n_chunks, fwd_body, r_f0, unroll=n_chunks <= 4)
        r_b = lax.fori_loop(0, n_chunks, bwd_body, r_b0, unroll=n_chunks <= 4)
        if not latent:
            st_ref[0, 0, hh] = r_f
            st_ref[0, 1, hh] = r_b

        def out_body(c, carry):
            qc = q_ref[0, chunk(c), lanes]
            kc = k_ref[0, chunk(c), lanes]
            vc = v_ref[0, chunk(c), lanes]
            a = lax.dot_general(qc, kc, (((1,), (1,)), ((), ())), preferred_element_type=F32) * d_in
            qd = (qc.astype(F32) * q_dec).astype(BF16)
            o = (jnp.dot(a.astype(BF16), vc, preferred_element_type=F32)
                 + jnp.dot(qd, rcat_scr[hh, c], preferred_element_type=F32))
            gate = g_ref[0, chunk(c), lanes].astype(F32)
            y = _rms(o, RET_DV) * gn * (gate * _sigmoid(gate))
            o_ref[0, chunk(c), lanes] = y.astype(BF16)
            return carry

        lax.fori_loop(0, n_chunks, out_body, 0, unroll=unroll)

    for hh in range(hpb):
        one_head(hh)


def _retention(rq, rk, rv, rg, ret_decay, ret_gn, state, layer):
    b_sz, n, _ = rq.shape
    latent = state is not None
    n_chunks = n // RET_CHUNK
    hpb = RET_HEADS if n_chunks <= 4 else 1
    tokw = pl.BlockSpec((1, n, hpb * HEAD_W), lambda b, h: (b, 0, h))
    in_specs = [pl.BlockSpec(memory_space=pltpu.SMEM), tokw, tokw, tokw, tokw, _resident(ret_gn.shape)]
    args = [ret_decay, rq, rk, rv, rg, ret_gn]
    if latent:
        in_specs.append(pl.BlockSpec((1, 1, 2, hpb, RET_DK, RET_DV), lambda b, h: (b, layer, 0, h, 0, 0)))
        args.append(state)
    out_specs = [tokw]
    out_shape = [jax.ShapeDtypeStruct((b_sz, n, RET_HEADS * HEAD_W), BF16)]
    if not latent:
        out_specs.append(pl.BlockSpec((1, 2, hpb, RET_DK, RET_DV), lambda b, h: (b, 0, h, 0, 0)))
        out_shape.append(jax.ShapeDtypeStruct((b_sz, 2, RET_HEADS, RET_DK, RET_DV), F32))
    return pl.pallas_call(
        functools.partial(_ret_kernel, latent=latent, n_chunks=n_chunks, hpb=hpb),
        grid=(b_sz, RET_HEADS // hpb), in_specs=in_specs, out_specs=out_specs, out_shape=out_shape,
        scratch_shapes=[pltpu.VMEM((hpb, n_chunks, RET_CHUNK, RET_DV), BF16),
                        pltpu.VMEM((hpb, n_chunks, RET_CHUNK, RET_DV), F32)],
        compiler_params=_params(2),
        name="retention_latent" if latent else "retention_context",
    )(*args)


def _ctxkeys_kernel(ckv_ref, kr_ref, w_kn, w_v, mkn_ref, mk_ref, mv_ref):
    ckvb = ckv_ref[0, 0].astype(BF16)
    kr = kr_ref[0, 0]
    k_nope = jnp.dot(ckvb, w_kn[...], preferred_element_type=F32)
    mv_ref[0, 0] = jnp.dot(ckvb, w_v[...], preferred_element_type=F32).astype(BF16)
    ss_kr = jnp.sum(kr * kr, axis=-1, keepdims=True)
    k_gain = mkn_ref[...]
    for hd in range(MLA_HEADS):
        kno = k_nope[:, hd * LANES:(hd + 1) * LANES]
        ss = jnp.sum(kno * kno, axis=-1, keepdims=True) + ss_kr
        rstd = lax.rsqrt(ss * (1.0 / MLA_QK) + EPS)
        mk_ref[0, 0, :, hd * MLA_PAD:hd * MLA_PAD + LANES] = (kno * rstd * k_gain[:, :LANES]).astype(BF16)
        mk_ref[0, 0, :, hd * MLA_PAD + LANES:(hd + 1) * MLA_PAD] = (kr * rstd * k_gain[:, LANES:]).astype(BF16)


def _context_keys(cache_ckv, cache_kr_pad, lws):
    b_sz, n_layers, p, _ = cache_ckv.shape
    outs = []
    for layer, lw in enumerate(lws):
        outs.append(pl.pallas_call(
            _ctxkeys_kernel,
            grid=(b_sz,),
            in_specs=[
                pl.BlockSpec((1, 1, p, MLA_KV_LORA), lambda b, layer=layer: (b, layer, 0, 0)),
                pl.BlockSpec((1, 1, p, LANES), lambda b, layer=layer: (b, layer, 0, 0)),
                _resident(lw["w_kn"].shape), _resident(lw["w_v"].shape), _resident(lw["mkn"].shape),
            ],
            out_specs=[pl.BlockSpec((1, 1, p, MLA_HEADS * MLA_PAD), lambda b: (b, 0, 0, 0)),
                       pl.BlockSpec((1, 1, p, MLA_HEADS * MLA_V), lambda b: (b, 0, 0, 0))],
            out_shape=[jax.ShapeDtypeStruct((b_sz, 1, p, MLA_HEADS * MLA_PAD), BF16),
                       jax.ShapeDtypeStruct((b_sz, 1, p, MLA_HEADS * MLA_V), BF16)],
            compiler_params=_params(1),
            name="mla_context_keys",
        )(cache_ckv, cache_kr_pad, lw["w_kn"], lw["w_v"], lw["mkn"]))
    return outs


def _post_kernel(x_ref, oa_ref, or_ref, om_ref, g_ref, mod_ref, n2_ref, wb_ref, wo_ref, wu_ref, wd_ref,
                 o_ref, *, ff_chunk):
    x = x_ref[0]
    mod = mod_ref[0, 0]
    d = x.shape[-1]
    merged = jnp.zeros(x.shape, F32)
    for i, br_ref in enumerate((oa_ref, or_ref, om_ref)):
        gate = _sigmoid(g_ref[0, :, i * d:(i + 1) * d].astype(F32))
        merged = merged + gate * jnp.dot(br_ref[0], wb_ref[i], preferred_element_type=F32)
    mix = jnp.dot(merged.astype(BF16), wo_ref[...], preferred_element_type=F32)
    x1 = x + mod[2:3] * mix
    h2 = (_rms(x1, d) * n2_ref[...] * (1.0 + mod[4:5]) + mod[3:4]).astype(BF16)
    acc = jnp.zeros(x.shape, F32)
    for c in range(D_FF // ff_chunk):
        u = jnp.dot(h2, wu_ref[:, c * ff_chunk:(c + 1) * ff_chunk], preferred_element_type=F32)
        u = jnp.square(jnp.maximum(u, 0.0)).astype(BF16)
        acc = acc + jnp.dot(u, wd_ref[c * ff_chunk:(c + 1) * ff_chunk, :], preferred_element_type=F32)
    o_ref[0] = x1 + mod[5:6] * acc


def _post(x, oa, orr, om, gates, mod, layer, lw, latent, tm):
    b_sz, n, d = x.shape
    tm = min(tm, n)

    def tok(width):
        return pl.BlockSpec((1, tm, width), lambda b, i: (b, i, 0))

    if latent:
        mod_spec = pl.BlockSpec((1, 1, 6, d), lambda b, i: (layer, 1 + b, 0, 0))
    else:
        mod_spec = pl.BlockSpec((1, 1, 6, d), lambda b, i: (layer, 0, 0, 0))
    weights = [lw["norm2"], lw["w_branch"], lw["w_out"], lw["w_up"], lw["w_down"]]
    return pl.pallas_call(
        functools.partial(_post_kernel, ff_chunk=1024),
        grid=(b_sz, n // tm),
        in_specs=[tok(d), tok(BRANCH_W), tok(BRANCH_W), tok(BRANCH_W), tok(N_BRANCH * d), mod_spec]
                 + [_resident(w.shape) for w in weights],
        out_specs=tok(d),
        out_shape=jax.ShapeDtypeStruct((b_sz, n, d), F32),
        compiler_params=_params(2),
        name="merge_mlp_latent" if latent else "merge_mlp_context",
    )(x, oa, orr, om, gates, mod, *weights)


def _rope_tables(n):
    pos = jnp.arange(n, dtype=jnp.int32)
    row = (pos // GRID_W).astype(F32)
    col = (pos % GRID_W).astype(F32)
    axis_dim = ROT_DIM // 2
    inv = ROPE_BASE ** (-jnp.arange(0, axis_dim, 2, dtype=F32) / axis_dim)
    ang_r = row[:, None] * inv[None, :]
    ang_c = col[:, None] * inv[None, :]
    zeros = jnp.zeros_like(ang_r)
    cos = jnp.concatenate([jnp.cos(ang_r)] * 2 + [jnp.cos(ang_c)] * 2, axis=-1)
    sin_a = jnp.concatenate([-jnp.sin(ang_r), zeros, -jnp.sin(ang_c), zeros], axis=-1)
    sin_b = jnp.concatenate([zeros, jnp.sin(ang_r), zeros, jnp.sin(ang_c)], axis=-1)
    reps = LANES // ROT_DIM
    return tuple(jnp.tile(t, (1, reps)) for t in (cos, sin_a, sin_b))


def _layer_weights(l, w_in, norm1, norm2, diff_qn, diff_kn, diff_lambda, diff_subln, ret_decay, ret_gn,
                   mla_qa_norm, w_mla_qb, mla_kva_norm, w_mla_kvb, mla_qn, mla_kn, w_branch, w_out, w_up,
                   w_down):
    d = w_in.shape[1]
    w = w_in[l]
    o_ret = 3 * BRANCH_W
    o_rk = o_ret + RET_HEADS * RET_DK
    o_rv = o_rk + RET_HEADS * RET_DK
    o_rg = o_rv + RET_HEADS * RET_DV
    o_qa = o_rg + RET_HEADS * RET_DV
    o_kva = o_qa + MLA_Q_LORA
    o_kr = o_kva + MLA_KV_LORA
    o_gate = o_kr + MLA_ROPE

    w_mla = jnp.concatenate([w[:, o_qa:o_gate], jnp.zeros((d, LANES - MLA_ROPE), w.dtype)], axis=-1)
    w_qb = jnp.pad(w_mla_qb[l].reshape(MLA_Q_LORA, MLA_HEADS, MLA_QK), ((0, 0), (0, 0), (0, MLA_PAD - MLA_QK)))
    w_kvb = w_mla_kvb[l].reshape(MLA_KV_LORA, MLA_HEADS, MLA_NOPE + MLA_V)

    def pad_gain(g):
        return jnp.pad(g, (0, MLA_PAD - MLA_QK))[None, :]

    return {
        "w_diff": w[:, :o_ret].astype(BF16),
        "w_ret": w[:, o_ret:o_qa].astype(BF16),
        "w_mla": w_mla.astype(BF16),
        "w_gate": w[:, o_gate:].astype(BF16),
        "w_qb": w_qb.reshape(MLA_Q_LORA, MLA_HEADS * MLA_PAD).astype(BF16),
        "w_kn": w_kvb[:, :, :MLA_NOPE].reshape(MLA_KV_LORA, MLA_HEADS * MLA_NOPE).astype(BF16),
        "w_v": w_kvb[:, :, MLA_NOPE:].reshape(MLA_KV_LORA, MLA_HEADS * MLA_V).astype(BF16),
        "norm1": norm1[l][None, :], "norm2": norm2[l][None, :],
        "dqn": jnp.tile(diff_qn[l], MXU_W // DIFF_DH)[None, :],
        "dkn": jnp.tile(diff_kn[l], MXU_W // DIFF_DH)[None, :],
        "qa_g": mla_qa_norm[l][None, :], "kva_g": mla_kva_norm[l][None, :],
        "mqn": pad_gain(mla_qn[l]), "mkn": pad_gain(mla_kn[l]),
        "lmb": diff_lambda[l], "subln": diff_subln[l][None, :],
        "ret_decay": ret_decay[l], "ret_gn": ret_gn[l][None, :],
        "w_branch": w_branch[l].astype(BF16), "w_out": w_out[l].astype(BF16),
        "w_up": w_up[l].astype(BF16), "w_down": w_down[l].astype(BF16),
    }


def _lambda_init(l):
    a, b, c = LAMBDA_INIT_BASE
    return a - b * math.exp(-c * l)


def _block(x, mod, layer, lw, rope_tabs, caches, tm_in, tm_post, tq):
    latent = rope_tabs is not None
    b_sz, n, d = x.shape

    def flat(a):
        return a if latent else a.reshape(1, b_sz * n, a.shape[-1])

    outs = [o.reshape(b_sz, n, o.shape[-1]) for o in _inproj(flat(x), mod, layer, lw, rope_tabs, tm_in)]
    dq, dk, dv, rq, rk, rv, rg, mq, mk, mv, gates = outs[:11]
    lam_init = _lambda_init(layer)
    diff_cache = caches["diff"] if latent else None
    mla_cache = caches["mla"][layer] if latent else None
    oa = _attention(dq, dk, dv, diff_cache, layer, diff=True, lam_init=lam_init, lmb=lw["lmb"],
                    subln=lw["subln"], tq=tq)
    om = _attention(mq, mk, mv, mla_cache, 0, diff=False, tq=4 * tq)
    ret = _retention(rq, rk, rv, rg, lw["ret_decay"], lw["ret_gn"], caches["state"] if latent else None, layer)
    x = _post(flat(x), flat(oa), flat(ret[0]), flat(om), flat(gates), mod, layer, lw, latent,
              tm_post).reshape(b_sz, n, d)
    if latent:
        return x, None
    return x, (outs[11], outs[12], outs[13], outs[14], ret[1])


def kernel(x_prompt, x_sample, cache_diff_k, cache_diff_v, cache_mla_ckv, cache_mla_krope, state_ret,
           c, c_ctx, w_mod, b_mod, norm1, norm2, w_in, diff_qn, diff_kn, diff_lambda, diff_subln,
           ret_decay, ret_gn, mla_qa_norm, w_mla_qb, mla_kva_norm, w_mla_kvb, mla_qn, mla_kn,
           w_branch, w_out, w_up, w_down):
    n_layers = w_in.shape[0]
    d = x_prompt.shape[-1]
    bd, n_lat = x_sample.shape[:2]
    b_ctx, n_ctx = x_prompt.shape[:2]
    p = cache_diff_k.shape[2]
    assert bd + 1 <= MOD_ROWS

    lws = [_layer_weights(l, w_in, norm1, norm2, diff_qn, diff_kn, diff_lambda, diff_subln, ret_decay, ret_gn,
                          mla_qa_norm, w_mla_qb, mla_kva_norm, w_mla_kvb, mla_qn, mla_kn, w_branch, w_out,
                          w_up, w_down) for l in range(n_layers)]

    cvec = jnp.concatenate([c_ctx[None, :], c, jnp.zeros((MOD_ROWS - 1 - bd, d), F32)], axis=0)
    mod = _modulation(cvec, w_mod, b_mod).reshape(n_layers, MOD_ROWS, 6, d)

    xp = x_prompt
    ctx_outs = []
    for l in range(n_layers):
        xp, ctx = _block(xp, mod, l, lws[l], None, None, 512, 512, 256)
        ctx_outs.append(ctx)
    new_diff_k = jnp.stack([o[0] for o in ctx_outs], axis=1).reshape(b_ctx, n_layers, n_ctx, 2 * DIFF_HEADS, DIFF_DH)
    new_diff_v = jnp.stack([o[1] for o in ctx_outs], axis=1).reshape(b_ctx, n_layers, n_ctx, DIFF_HEADS, 2 * DIFF_DH)
    new_mla_ckv = jnp.stack([o[2] for o in ctx_outs], axis=1)
    new_mla_krope = jnp.stack([o[3] for o in ctx_outs], axis=1)
    new_state_ret = jnp.stack([o[4] for o in ctx_outs], axis=1)

    caches = {
        "diff": (cache_diff_k.reshape(bd, n_layers, p, 2 * DIFF_HEADS * DIFF_DH),
                 cache_diff_v.reshape(bd, n_layers, p, DIFF_HEADS * 2 * DIFF_DH)),
        "mla": _context_keys(cache_mla_ckv, jnp.pad(cache_mla_krope, ((0, 0),) * 3 + ((0, LANES - MLA_ROPE),)), lws),
        "state": state_ret,
    }
    rope_tabs = _rope_tables(n_lat)
    xs = x_sample
    for l in range(n_layers):
        xs, _ = _block(xs, mod, l, lws[l], rope_tabs, caches, 512, 512, 256)

    return (xp, xs, new_diff_k, new_diff_v, new_mla_ckv, new_mla_krope, new_state_ret)
```

```python
import functools
import math

import jax
import jax.numpy as jnp
from jax import lax
from jax.experimental import pallas as pl
from jax.experimental.pallas import tpu as pltpu

F32 = jnp.float32
BF16 = jnp.bfloat16

D_MODEL = 1024
GRID_W = 64
DIFF_HEADS = 4
DIFF_DH = 64
RET_HEADS = 4
RET_DK = 64
RET_DV = 128
RET_CHUNK = 128
MLA_HEADS = 4
MLA_Q_LORA = 384
MLA_KV_LORA = 256
MLA_NOPE = 128
MLA_ROPE = 64
MLA_V = 128
MLA_QK = MLA_NOPE + MLA_ROPE
N_BRANCH = 3
BRANCH_W = 512
D_FF = 4 * D_MODEL
ROT_DIM = 64
ROPE_BASE = 10000.0
EPS = 1e-6
LAMBDA_INIT_BASE = (0.8, 0.6, 0.3)

LANES = 128
MXU_W = 256
MLA_PAD = 2 * LANES
HEAD_W = 128
LOG2E = 1.4426950408889634
VMEM_LIMIT = 56 * 1024 * 1024
_IDLE = object()
MOD_ROWS = 16


def _resident(shape):
    nd = len(shape)
    return pl.BlockSpec(shape, lambda *_: (0,) * nd, pipeline_mode=pl.Buffered(1))


def _params(n_grid):
    return pltpu.CompilerParams(dimension_semantics=("arbitrary",) * n_grid,
                                vmem_limit_bytes=VMEM_LIMIT)


def _sigmoid(x):
    return 1.0 / (1.0 + jnp.exp(-x))


def _rms(x, width):
    ss = jnp.sum(x * x, axis=-1, keepdims=True)
    return x * lax.rsqrt(ss * (1.0 / width) + EPS)


def _mod_kernel(c_ref, w_ref, b_ref, o_ref):
    c = c_ref[...]
    a = (c * _sigmoid(c)).astype(BF16)
    part = jnp.dot(a, w_ref[0].astype(BF16), preferred_element_type=F32)

    @pl.when(pl.program_id(1) == 0)
    def _():
        o_ref[0] = part + b_ref[0]

    @pl.when(pl.program_id(1) > 0)
    def _():
        o_ref[0] += part


def _modulation(cvec, w_mod, b_mod):
    n_layers, d, d6 = w_mod.shape
    tk = d // 4
    return pl.pallas_call(
        _mod_kernel,
        grid=(n_layers, d // tk),
        in_specs=[
            pl.BlockSpec((MOD_ROWS, tk), lambda l, k: (0, k)),
            pl.BlockSpec((1, tk, d6), lambda l, k: (l, k, 0)),
            pl.BlockSpec((1, 1, d6), lambda l, k: (l, 0, 0)),
        ],
        out_specs=pl.BlockSpec((1, MOD_ROWS, d6), lambda l, k: (l, 0, 0)),
        out_shape=jax.ShapeDtypeStruct((n_layers, MOD_ROWS, d6), F32),
        compiler_params=_params(2),
        name="modulation",
    )(cvec, w_mod, b_mod.reshape(n_layers, 1, d6))


def _inproj_kernel(*refs, latent):
    it = iter(refs)
    x_ref, mod_ref, n1_ref = next(it), next(it), next(it)
    if latent:
        cos_ref, sa_ref, sb_ref = next(it), next(it), next(it)
    w_diff, w_ret, w_mla, w_gate, w_qb, w_kn, w_v = (next(it) for _ in range(7))
    dqn_ref, dkn_ref, qag_ref, kvag_ref, mqn_ref, mkn_ref = (next(it) for _ in range(6))
    (dq_ref, dk_ref, dv_ref, rq_ref, rk_ref, rv_ref, rg_ref,
     mq_ref, mk_ref, mv_ref, gate_ref) = (next(it) for _ in range(11))
    if not latent:
        odk_ref, odv_ref, ockv_ref, okr_ref = (next(it) for _ in range(4))

    x = x_ref[0]
    mod = mod_ref[0, 0]
    h = _rms(x, D_MODEL) * n1_ref[...] * (1.0 + mod[1:2]) + mod[0:1]
    hb = h.astype(BF16)

    if latent:
        cos, sa, sb = cos_ref[...], sa_ref[...], sb_ref[...]

        def rope(y):
            return (y * cos + pltpu.roll(y, LANES - 16, 1) * sa + pltpu.roll(y, 16, 1) * sb)
    else:
        def rope(y):
            return y

    def group(z, g):
        return z[:, g * LANES:(g + 1) * LANES]

    r_i = lax.broadcasted_iota(jnp.int32, (MXU_W, MXU_W), 0) // DIFF_DH
    c_i = lax.broadcasted_iota(jnp.int32, (MXU_W, MXU_W), 1) // DIFF_DH
    seg = jnp.where(r_i == c_i, 1.0, 0.0).astype(BF16)

    def qk_norm(y, gain):
        ss = jnp.dot((y * y).astype(BF16), seg, preferred_element_type=F32)
        return y * lax.rsqrt(ss * (1.0 / DIFF_DH) + EPS) * gain

    zd = jnp.dot(hb, w_diff[...], preferred_element_type=F32)
    n_g = BRANCH_W // LANES
    per_tile = MXU_W // LANES
    q_gain = dqn_ref[...] * (DIFF_DH ** -0.5 * LOG2E)
    for g2 in range(BRANCH_W // MXU_W):
        qn = qk_norm(zd[:, g2 * MXU_W:(g2 + 1) * MXU_W], q_gain)
        kn = qk_norm(zd[:, BRANCH_W + g2 * MXU_W:BRANCH_W + (g2 + 1) * MXU_W], dkn_ref[...])
        for g in range(g2 * per_tile, (g2 + 1) * per_tile):
            sl = slice(g * LANES, (g + 1) * LANES)
            dq_ref[0, :, sl] = rope(group(qn, g % per_tile)).astype(BF16)
            dk_ref[0, :, sl] = rope(group(kn, g % per_tile)).astype(BF16)
            v = group(zd, 2 * n_g + g)
            dv_ref[0, :, sl] = v.astype(BF16)
            if not latent:
                odk_ref[0, :, sl] = group(kn, g % per_tile)
                odv_ref[0, :, sl] = v

    zr = jnp.dot(hb, w_ret[...], preferred_element_type=F32)
    low = lax.broadcasted_iota(jnp.int32, (x.shape[0], LANES), 1) < RET_DK

    def twice(y):
        swapped = pltpu.roll(y, RET_DK, 1)
        return jnp.where(low, y, swapped), jnp.where(low, swapped, y)

    n_p = RET_HEADS * RET_DK // LANES
    for p in range(n_p):
        q_pair = twice(rope(group(zr, p)))
        k_pair = twice(rope(group(zr, n_p + p)) * RET_DK ** -0.5)
        for i in range(2):
            sl = slice((2 * p + i) * LANES, (2 * p + i + 1) * LANES)
            rq_ref[0, :, sl] = q_pair[i].astype(BF16)
            rk_ref[0, :, sl] = k_pair[i].astype(BF16)
    for g in range(n_g):
        sl = slice(g * LANES, (g + 1) * LANES)
        rv_ref[0, :, sl] = group(zr, 2 * n_p + g).astype(BF16)
        rg_ref[0, :, sl] = group(zr, 2 * n_p + n_g + g).astype(BF16)

    zm = jnp.dot(hb, w_mla[...], preferred_element_type=F32)
    qa = _rms(zm[:, :MLA_Q_LORA], MLA_Q_LORA) * qag_ref[...]
    mqz = jnp.dot(qa.astype(BF16), w_qb[...], preferred_element_type=F32)
    ckv = _rms(zm[:, MLA_Q_LORA:MLA_Q_LORA + MLA_KV_LORA], MLA_KV_LORA) * kvag_ref[...]
    ckvb = ckv.astype(BF16)
    k_nope = jnp.dot(ckvb, w_kn[...], preferred_element_type=F32)
    mv_ref[0] = jnp.dot(ckvb, w_v[...], preferred_element_type=F32).astype(BF16)
    kr = zm[:, MLA_Q_LORA + MLA_KV_LORA:]
    ss_kr = jnp.sum(kr * kr, axis=-1, keepdims=True)
    if not latent:
        ockv_ref[0] = ckv
        okr_ref[0] = kr[:, :MLA_ROPE]
    q_gain = mqn_ref[...] * (MLA_QK ** -0.5 * LOG2E)
    k_gain = mkn_ref[...]
    for hd in range(MLA_HEADS):
        lo = slice(hd * MLA_PAD, hd * MLA_PAD + LANES)
        hi = slice(hd * MLA_PAD + LANES, (hd + 1) * MLA_PAD)
        qh = _rms(mqz[:, hd * MLA_PAD:(hd + 1) * MLA_PAD], MLA_QK) * q_gain
        mq_ref[0, :, lo] = qh[:, :LANES].astype(BF16)
        mq_ref[0, :, hi] = rope(qh[:, LANES:]).astype(BF16)
        kno = k_nope[:, hd * LANES:(hd + 1) * LANES]
        ss = jnp.sum(kno * kno, axis=-1, keepdims=True) + ss_kr
        rstd = lax.rsqrt(ss * (1.0 / MLA_QK) + EPS)
        mk_ref[0, :, lo] = (kno * rstd * k_gain[:, :LANES]).astype(BF16)
        mk_ref[0, :, hi] = rope(kr * rstd * k_gain[:, LANES:]).astype(BF16)

    gate_ref[0] = jnp.dot(hb, w_gate[...], preferred_element_type=F32).astype(BF16)


def _inproj(x, mod, layer, lw, rope_tabs, tm):
    b_sz, n, d = x.shape
    latent = rope_tabs is not None
    tm = min(tm, n)
    grid = (b_sz, n // tm)

    def tok(width):
        return pl.BlockSpec((1, tm, width), lambda b, i: (b, i, 0))

    if latent:
        mod_spec = pl.BlockSpec((1, 1, 6, d), lambda b, i: (layer, 1 + b, 0, 0))
    else:
        mod_spec = pl.BlockSpec((1, 1, 6, d), lambda b, i: (layer, 0, 0, 0))
    in_specs = [tok(d), mod_spec, _resident((1, d))]
    args = [x, mod, lw["norm1"]]
    if latent:
        in_specs += [pl.BlockSpec((tm, LANES), lambda b, i: (i, 0))] * 3
        args += list(rope_tabs)
    weights = [lw["w_diff"], lw["w_ret"], lw["w_mla"], lw["w_gate"], lw["w_qb"], lw["w_kn"], lw["w_v"]]
    gains = [lw["dqn"], lw["dkn"], lw["qa_g"], lw["kva_g"], lw["mqn"], lw["mkn"]]
    for a in weights + gains:
        in_specs.append(_resident(a.shape))
        args.append(a)

    widths = [BRANCH_W] * 7 + [MLA_HEADS * MLA_PAD] * 2 + [MLA_HEADS * MLA_V, N_BRANCH * d]
    out_specs = [tok(w) for w in widths]
    out_shape = [jax.ShapeDtypeStruct((b_sz, n, w), BF16) for w in widths]
    if not latent:
        for w in (BRANCH_W, BRANCH_W, MLA_KV_LORA, MLA_ROPE):
            out_specs.append(tok(w))
            out_shape.append(jax.ShapeDtypeStruct((b_sz, n, w), F32))
    return pl.pallas_call(
        functools.partial(_inproj_kernel, latent=latent),
        grid=grid, in_specs=in_specs, out_specs=out_specs, out_shape=out_shape,
        compiler_params=_params(2),
        name="inproj_latent" if latent else "inproj_context",
    )(*args)


def _attn_kernel(*refs, diff, has_cache, n_new, n_past, tq, ck, hpb, lam_init):
    it = iter(refs)
    q_ref, kn_ref, vn_ref = next(it), next(it), next(it)
    if has_cache:
        kc_ref, vc_ref = next(it), next(it)
    if diff:
        lmb_ref, sub_ref = next(it), next(it)
    o_ref = next(it)
    s_bufs = (next(it), next(it))
    q_tiles = n_new // tq
    n_tiles = hpb * q_tiles
    dq = q_ref.shape[-1] // hpb

    def chunks_of(head):
        kl = slice(head * dq, (head + 1) * dq)
        vl = slice(head * HEAD_W, (head + 1) * HEAD_W)
        out = [(lambda j=j: kn_ref[0, j * ck:(j + 1) * ck, kl], lambda j=j: vn_ref[0, j * ck:(j + 1) * ck, vl])
               for j in range(n_new // ck)]
        if has_cache:
            out += [(lambda j=j: kc_ref[0, 0, j * ck:(j + 1) * ck, kl].astype(BF16),
                     lambda j=j: vc_ref[0, 0, j * ck:(j + 1) * ck, vl].astype(BF16))
                    for j in range(n_past // ck)]
        return out

    def place(t):
        if hpb == 1:
            return 0, pl.ds(pl.multiple_of(t * tq, tq), tq)
        return t // q_tiles, slice((t % q_tiles) * tq, (t % q_tiles + 1) * tq)

    rows = 2 * tq if diff else tq

    def cols(j):
        return slice(j * ck, (j + 1) * ck)


    def score_stage(t, slot, _, split):
        head, q_rows = place(t)
        q = q_ref[0, q_rows, head * dq:(head + 1) * dq]
        if diff:
            lane = lax.broadcasted_iota(jnp.int32, q.shape, 1)
            qf = q.astype(F32)
            q = jnp.concatenate([jnp.where(lane < DIFF_DH, qf, 0.0), jnp.where(lane >= DIFF_DH, qf, 0.0)],
                                axis=0).astype(BF16)
        m_acc = jnp.full((rows, LANES), -jnp.inf, F32)
        for j, (k_of, _) in enumerate(chunks_of(head)):
            s = lax.dot_general(q, k_of(), (((1,), (1,)), ((), ())), preferred_element_type=F32)
            s_bufs[slot][:, cols(j)] = s
            for u in range(ck // LANES):
                m_acc = jnp.maximum(m_acc, s[:, u * LANES:(u + 1) * LANES])
            yield
        return jnp.broadcast_to(jnp.max(m_acc, axis=-1, keepdims=True), (rows, LANES))

    def slab(j, u):
        return slice(j * ck + u * LANES, j * ck + (u + 1) * LANES)

    def exp_of(slot, j, m):
        return [jnp.exp2(s_bufs[slot][:, slab(j, u)] - m) for u in range(ck // LANES)]

    def softmax_value_stage(t, slot, m, split):
        head, q_rows = place(t)
        n_part = 2 if split else 1
        part = rows // n_part
        accs = [jnp.zeros((part, HEAD_W + LANES), F32)] * n_part
        ones = jnp.ones((ck, LANES), BF16)
        for j, (_, v_of) in enumerate(chunks_of(head)):
            e = jnp.concatenate([x.astype(BF16) for x in exp_of(slot, j, m)], axis=1)
            v1 = jnp.concatenate([v_of(), ones], axis=1)
            accs = [accs[i] + jnp.dot(e[i * part:(i + 1) * part], v1, preferred_element_type=F32)
                    for i in range(n_part)]
            yield
        acc = jnp.concatenate(accs, axis=0)
        o = acc[:, :HEAD_W] / acc[:, HEAD_W:]
        if diff:
            lmb = lmb_ref[...]
            lam = (jnp.exp(jnp.sum(lmb[0:1] * lmb[1:2], axis=-1, keepdims=True))
                   - jnp.exp(jnp.sum(lmb[2:3] * lmb[3:4], axis=-1, keepdims=True)) + lam_init)
            o = o[:tq] - lam * o[tq:]
            o = _rms(o, HEAD_W) * sub_ref[...] * (1.0 - lam_init)
        o_ref[0, q_rows, head * HEAD_W:(head + 1) * HEAD_W] = o.astype(BF16)

    stage_fns = (score_stage, softmax_value_stage)
    n_stage = len(stage_fns)
    n_slot = 2

    def step(u, u_static, inputs, split):
        gens = [(k, fn(u - k, (u_static - k) % n_slot, inputs[k], split))
                for k, fn in enumerate(stage_fns) if inputs[k] is not _IDLE]
        new = {}
        while gens:
            for item in list(gens):
                try:
                    next(item[1])
                except StopIteration as done:
                    new[item[0]] = done.value
                    gens.remove(item)
        return new

    def inputs_of(results, u):
        return [(results[k - 1] if k else None) if 0 <= u - k < n_tiles else _IDLE for k in range(n_stage)]

    n_steps = n_tiles + n_stage - 1
    steady = [u for u in range(n_steps) if _IDLE not in inputs_of([None] * n_stage, u)]
    n_iter = len(steady) // n_slot
    rolled = set(steady[:n_iter * n_slot]) if n_iter >= 2 else set()
    assert not rolled or hpb == 1

    results = [None] * n_stage
    u = 0
    while u < n_steps:
        if u in rolled:
            def body(i, res, u0=u):
                res = list(res) + [None]
                for r in range(n_slot):
                    new = step(u0 + i * n_slot + r, u0 + r, inputs_of(res, u0 + r), False)
                    res = [new[k] for k in range(n_stage)]
                return tuple(res[:-1])

            results = list(lax.fori_loop(0, n_iter, body, tuple(results[:-1]))) + [None]
            u += n_iter * n_slot
        else:
            new = step(u, u, inputs_of(results, u), bool(rolled) and u > max(rolled))
            results = [new.get(k) for k in range(n_stage)]
            u += 1


def _attention(q, k_new, v_new, cache, layer, *, diff, lam_init=0.0, lmb=None, subln=None, tq=256, ck=512):
    b_sz, n, _ = q.shape
    dq = LANES if diff else MLA_PAD
    n_heads = DIFF_HEADS if diff else MLA_HEADS
    tq = min(tq, n)
    has_cache = cache is not None
    n_past = cache[0].shape[2] if has_cache else 0
    ck = min(ck, n, n_past) if has_cache else min(ck, n)
    hpb = n_heads if n // tq == 1 else 1

    in_specs = [
        pl.BlockSpec((1, n, hpb * dq), lambda b, h: (b, 0, h)),
        pl.BlockSpec((1, n, hpb * dq), lambda b, h: (b, 0, h)),
        pl.BlockSpec((1, n, hpb * HEAD_W), lambda b, h: (b, 0, h)),
    ]
    args = [q, k_new, v_new]
    if has_cache:
        in_specs += [
            pl.BlockSpec((1, 1, n_past, hpb * dq), lambda b, h: (b, layer, 0, h)),
            pl.BlockSpec((1, 1, n_past, hpb * HEAD_W), lambda b, h: (b, layer, 0, h)),
        ]
        args += list(cache)
    if diff:
        in_specs += [_resident(lmb.shape), _resident(subln.shape)]
        args += [lmb, subln]
    rows = 2 * tq if diff else tq
    return pl.pallas_call(
        functools.partial(_attn_kernel, diff=diff, has_cache=has_cache, n_new=n, n_past=n_past,
                          tq=tq, ck=ck, hpb=hpb, lam_init=lam_init),
        grid=(b_sz, n_heads // hpb), in_specs=in_specs,
        out_specs=pl.BlockSpec((1, n, hpb * HEAD_W), lambda b, h: (b, 0, h)),
        out_shape=jax.ShapeDtypeStruct((b_sz, n, n_heads * HEAD_W), BF16),
        scratch_shapes=[pltpu.VMEM((rows, n + n_past), F32)] * 2,
        compiler_params=_params(2),
        name=("diff_attn" if diff else "mla_attn") + ("_latent" if has_cache else "_context"),
    )(*args)


def _log_sigmoid(x):
    return jnp.minimum(x, 0.0) - jnp.log(1.0 + jnp.exp(-jnp.abs(x)))


def _ret_kernel(*refs, latent, n_chunks, hpb):
    it = iter(refs)
    dec_ref, q_ref, k_ref, v_ref, g_ref, gn_ref = (next(it) for _ in range(6))
    if latent:
        r0_ref = next(it)
    o_ref = next(it)
    if not latent:
        st_ref = next(it)
    rcat_scr, u_scr = next(it), next(it)

    c_len = RET_CHUNK
    row = lax.broadcasted_iota(jnp.int32, (c_len, c_len), 0).astype(F32)
    col = lax.broadcasted_iota(jnp.int32, (c_len, c_len), 1).astype(F32)
    lane_fwd = lax.broadcasted_iota(jnp.int32, (c_len, 2 * RET_DK), 1) < RET_DK
    gn = gn_ref[...]
    unroll = min(16, n_chunks)

    def chunk(c):
        return pl.ds(pl.multiple_of(c * c_len, c_len), c_len)

    def one_head(hh):
        hd = pl.program_id(1) * hpb + hh
        lanes = slice(hh * HEAD_W, (hh + 1) * HEAD_W)
        lg_f = _log_sigmoid(jnp.full((1, 1), dec_ref[0, hd], F32))
        lg_b = _log_sigmoid(jnp.full((1, 1), dec_ref[1, hd], F32))
        d_in = 0.5 * jnp.where(row >= col, jnp.exp((row - col) * lg_f), jnp.exp((col - row) * lg_b))
        q_dec = jnp.where(lane_fwd, jnp.exp((row + 1.0) * lg_f), jnp.exp((c_len - row) * lg_b))
        k_dec = jnp.where(lane_fwd, jnp.exp((c_len - 1.0 - row) * lg_f), jnp.exp(row * lg_b))
        dec_f = jnp.exp(c_len * lg_f)
        dec_b = jnp.exp(c_len * lg_b)

        def inc_body(c, carry):
            kd = (k_ref[0, chunk(c), lanes].astype(F32) * k_dec).astype(BF16)
            u_scr[hh, c] = lax.dot_general(kd, v_ref[0, chunk(c), lanes], (((0,), (0,)), ((), ())),
                                           preferred_element_type=F32)
            return carry

        lax.fori_loop(0, n_chunks, inc_body, 0, unroll=unroll)

        if latent:
            r_f0 = r0_ref[0, 0, 0, hh]
            r_b0 = r0_ref[0, 0, 1, hh]
        else:
            r_f0 = jnp.zeros((RET_DK, RET_DV), F32)
            r_b0 = r_f0

        def fwd_body(c, r):
            rcat_scr[hh, c, 0:RET_DK, :] = r.astype(BF16)
            return dec_f * r + u_scr[hh, c, 0:RET_DK, :]

        def bwd_body(t, r):
            c = n_chunks - 1 - t
            rcat_scr[hh, c, RET_DK:2 * RET_DK, :] = r.astype(BF16)
            return dec_b * r + u_scr[hh, c, RET_DK:2 * RET_DK, :]

        r_f = lax.fori_loop(0, n_chunks, fwd_body, r_f0, unroll=n_chunks <= 4)
        r_b = lax.fori_loop(0, n_chunks, bwd_body, r_b0, unroll=n_chunks <= 4)
        if not latent:
            st_ref[0, 0, hh] = r_f
            st_ref[0, 1, hh] = r_b

        def out_body(c, carry):
            qc = q_ref[0, chunk(c), lanes]
            kc = k_ref[0, chunk(c), lanes]
            vc = v_ref[0, chunk(c), lanes]
            a = lax.dot_general(qc, kc, (((1,), (1,)), ((), ())), preferred_element_type=F32) * d_in
            qd = (qc.astype(F32) * q_dec).astype(BF16)
            o = (jnp.dot(a.astype(BF16), vc, preferred_element_type=F32)
                 + jnp.dot(qd, rcat_scr[hh, c], preferred_element_type=F32))
            gate = g_ref[0, chunk(c), lanes].astype(F32)
            y = _rms(o, RET_DV) * gn * (gate * _sigmoid(gate))
            o_ref[0, chunk(c), lanes] = y.astype(BF16)
            return carry

        lax.fori_loop(0, n_chunks, out_body, 0, unroll=unroll)

    for hh in range(hpb):
        one_head(hh)


def _retention(rq, rk, rv, rg, ret_decay, ret_gn, state, layer):
    b_sz, n, _ = rq.shape
    latent = state is not None
    n_chunks = n // RET_CHUNK
    hpb = RET_HEADS if n_chunks <= 4 else 1
    tokw = pl.BlockSpec((1, n, hpb * HEAD_W), lambda b, h: (b, 0, h))
    in_specs = [pl.BlockSpec(memory_space=pltpu.SMEM), tokw, tokw, tokw, tokw, _resident(ret_gn.shape)]
    args = [ret_decay, rq, rk, rv, rg, ret_gn]
    if latent:
        in_specs.append(pl.BlockSpec((1, 1, 2, hpb, RET_DK, RET_DV), lambda b, h: (b, layer, 0, h, 0, 0)))
        args.append(state)
    out_specs = [tokw]
    out_shape = [jax.ShapeDtypeStruct((b_sz, n, RET_HEADS * HEAD_W), BF16)]
    if not latent:
        out_specs.append(pl.BlockSpec((1, 2, hpb, RET_DK, RET_DV), lambda b, h: (b, 0, h, 0, 0)))
        out_shape.append(jax.ShapeDtypeStruct((b_sz, 2, RET_HEADS, RET_DK, RET_DV), F32))
    return pl.pallas_call(
        functools.partial(_ret_kernel, latent=latent, n_chunks=n_chunks, hpb=hpb),
        grid=(b_sz, RET_HEADS // hpb), in_specs=in_specs, out_specs=out_specs, out_shape=out_shape,
        scratch_shapes=[pltpu.VMEM((hpb, n_chunks, RET_CHUNK, RET_DV), BF16),
                        pltpu.VMEM((hpb, n_chunks, RET_CHUNK, RET_DV), F32)],
        compiler_params=_params(2),
        name="retention_latent" if latent else "retention_context",
    )(*args)


def _ctxkeys_kernel(ckv_ref, kr_ref, w_kn, w_v, mkn_ref, mk_ref, mv_ref):
    ckvb = ckv_ref[0, 0].astype(BF16)
    kr = kr_ref[0, 0]
    k_nope = jnp.dot(ckvb, w_kn[...], preferred_element_type=F32)
    mv_ref[0, 0] = jnp.dot(ckvb, w_v[...], preferred_element_type=F32).astype(BF16)
    ss_kr = jnp.sum(kr * kr, axis=-1, keepdims=True)
    k_gain = mkn_ref[...]
    for hd in range(MLA_HEADS):
        kno = k_nope[:, hd * LANES:(hd + 1) * LANES]
        ss = jnp.sum(kno * kno, axis=-1, keepdims=True) + ss_kr
        rstd = lax.rsqrt(ss * (1.0 / MLA_QK) + EPS)
        mk_ref[0, 0, :, hd * MLA_PAD:hd * MLA_PAD + LANES] = (kno * rstd * k_gain[:, :LANES]).astype(BF16)
        mk_ref[0, 0, :, hd * MLA_PAD + LANES:(hd + 1) * MLA_PAD] = (kr * rstd * k_gain[:, LANES:]).astype(BF16)


def _context_keys(cache_ckv, cache_kr_pad, lws):
    b_sz, n_layers, p, _ = cache_ckv.shape
    outs = []
    for layer, lw in enumerate(lws):
        outs.append(pl.pallas_call(
            _ctxkeys_kernel,
            grid=(b_sz,),
            in_specs=[
                pl.BlockSpec((1, 1, p, MLA_KV_LORA), lambda b, layer=layer: (b, layer, 0, 0)),
                pl.BlockSpec((1, 1, p, LANES), lambda b, layer=layer: (b, layer, 0, 0)),
                _resident(lw["w_kn"].shape), _resident(lw["w_v"].shape), _resident(lw["mkn"].shape),
            ],
            out_specs=[pl.BlockSpec((1, 1, p, MLA_HEADS * MLA_PAD), lambda b: (b, 0, 0, 0)),
                       pl.BlockSpec((1, 1, p, MLA_HEADS * MLA_V), lambda b: (b, 0, 0, 0))],
            out_shape=[jax.ShapeDtypeStruct((b_sz, 1, p, MLA_HEADS * MLA_PAD), BF16),
                       jax.ShapeDtypeStruct((b_sz, 1, p, MLA_HEADS * MLA_V), BF16)],
            compiler_params=_params(1),
            name="mla_context_keys",
        )(cache_ckv, cache_kr_pad, lw["w_kn"], lw["w_v"], lw["mkn"]))
    return outs


def _post_kernel(x_ref, oa_ref, or_ref, om_ref, g_ref, mod_ref, n2_ref, wb_ref, wo_ref, wu_ref, wd_ref,
                 o_ref, *, ff_chunk):
    x = x_ref[0]
    mod = mod_ref[0, 0]
    d = x.shape[-1]
    merged = jnp.zeros(x.shape, F32)
    for i, br_ref in enumerate((oa_ref, or_ref, om_ref)):
        gate = _sigmoid(g_ref[0, :, i * d:(i + 1) * d].astype(F32))
        merged = merged + gate * jnp.dot(br_ref[0], wb_ref[i], preferred_element_type=F32)
    mix = jnp.dot(merged.astype(BF16), wo_ref[...], preferred_element_type=F32)
    x1 = x + mod[2:3] * mix
    h2 = (_rms(x1, d) * n2_ref[...] * (1.0 + mod[4:5]) + mod[3:4]).astype(BF16)
    acc = jnp.zeros(x.shape, F32)
    for c in range(D_FF // ff_chunk):
        u = jnp.dot(h2, wu_ref[:, c * ff_chunk:(c + 1) * ff_chunk], preferred_element_type=F32)
        u = jnp.square(jnp.maximum(u, 0.0)).astype(BF16)
        acc = acc + jnp.dot(u, wd_ref[c * ff_chunk:(c + 1) * ff_chunk, :], preferred_element_type=F32)
    o_ref[0] = x1 + mod[5:6] * acc


def _post(x, oa, orr, om, gates, mod, layer, lw, latent, tm):
    b_sz, n, d = x.shape
    tm = min(tm, n)

    def tok(width):
        return pl.BlockSpec((1, tm, width), lambda b, i: (b, i, 0))

    if latent:
        mod_spec = pl.BlockSpec((1, 1, 6, d), lambda b, i: (layer, 1 + b, 0, 0))
    else:
        mod_spec = pl.BlockSpec((1, 1, 6, d), lambda b, i: (layer, 0, 0, 0))
    weights = [lw["norm2"], lw["w_branch"], lw["w_out"], lw["w_up"], lw["w_down"]]
    return pl.pallas_call(
        functools.partial(_post_kernel, ff_chunk=1024),
        grid=(b_sz, n // tm),
        in_specs=[tok(d), tok(BRANCH_W), tok(BRANCH_W), tok(BRANCH_W), tok(N_BRANCH * d), mod_spec]
                 + [_resident(w.shape) for w in weights],
        out_specs=tok(d),
        out_shape=jax.ShapeDtypeStruct((b_sz, n, d), F32),
        compiler_params=_params(2),
        name="merge_mlp_latent" if latent else "merge_mlp_context",
    )(x, oa, orr, om, gates, mod, *weights)


def _rope_tables(n):
    pos = jnp.arange(n, dtype=jnp.int32)
    row = (pos // GRID_W).astype(F32)
    col = (pos % GRID_W).astype(F32)
    axis_dim = ROT_DIM // 2
    inv = ROPE_BASE ** (-jnp.arange(0, axis_dim, 2, dtype=F32) / axis_dim)
    ang_r = row[:, None] * inv[None, :]
    ang_c = col[:, None] * inv[None, :]
    zeros = jnp.zeros_like(ang_r)
    cos = jnp.concatenate([jnp.cos(ang_r)] * 2 + [jnp.cos(ang_c)] * 2, axis=-1)
    sin_a = jnp.concatenate([-jnp.sin(ang_r), zeros, -jnp.sin(ang_c), zeros], axis=-1)
    sin_b = jnp.concatenate([zeros, jnp.sin(ang_r), zeros, jnp.sin(ang_c)], axis=-1)
    reps = LANES // ROT_DIM
    return tuple(jnp.tile(t, (1, reps)) for t in (cos, sin_a, sin_b))


def _layer_weights(l, w_in, norm1, norm2, diff_qn, diff_kn, diff_lambda, diff_subln, ret_decay, ret_gn,
                   mla_qa_norm, w_mla_qb, mla_kva_norm, w_mla_kvb, mla_qn, mla_kn, w_branch, w_out, w_up,
                   w_down):
    d = w_in.shape[1]
    w = w_in[l]
    o_ret = 3 * BRANCH_W
    o_rk = o_ret + RET_HEADS * RET_DK
    o_rv = o_rk + RET_HEADS * RET_DK
    o_rg = o_rv + RET_HEADS * RET_DV
    o_qa = o_rg + RET_HEADS * RET_DV
    o_kva = o_qa + MLA_Q_LORA
    o_kr = o_kva + MLA_KV_LORA
    o_gate = o_kr + MLA_ROPE

    w_mla = jnp.concatenate([w[:, o_qa:o_gate], jnp.zeros((d, LANES - MLA_ROPE), w.dtype)], axis=-1)
    w_qb = jnp.pad(w_mla_qb[l].reshape(MLA_Q_LORA, MLA_HEADS, MLA_QK), ((0, 0), (0, 0), (0, MLA_PAD - MLA_QK)))
    w_kvb = w_mla_kvb[l].reshape(MLA_KV_LORA, MLA_HEADS, MLA_NOPE + MLA_V)

    def pad_gain(g):
        return jnp.pad(g, (0, MLA_PAD - MLA_QK))[None, :]

    return {
        "w_diff": w[:, :o_ret].astype(BF16),
        "w_ret": w[:, o_ret:o_qa].astype(BF16),
        "w_mla": w_mla.astype(BF16),
        "w_gate": w[:, o_gate:].astype(BF16),
        "w_qb": w_qb.reshape(MLA_Q_LORA, MLA_HEADS * MLA_PAD).astype(BF16),
        "w_kn": w_kvb[:, :, :MLA_NOPE].reshape(MLA_KV_LORA, MLA_HEADS * MLA_NOPE).astype(BF16),
        "w_v": w_kvb[:, :, MLA_NOPE:].reshape(MLA_KV_LORA, MLA_HEADS * MLA_V).astype(BF16),
        "norm1": norm1[l][None, :], "norm2": norm2[l][None, :],
        "dqn": jnp.tile(diff_qn[l], MXU_W // DIFF_DH)[None, :],
        "dkn": jnp.tile(diff_kn[l], MXU_W // DIFF_DH)[None, :],
        "qa_g": mla_qa_norm[l][None, :], "kva_g": mla_kva_norm[l][None, :],
        "mqn": pad_gain(mla_qn[l]), "mkn": pad_gain(mla_kn[l]),
        "lmb": diff_lambda[l], "subln": diff_subln[l][None, :],
        "ret_decay": ret_decay[l], "ret_gn": ret_gn[l][None, :],
        "w_branch": w_branch[l].astype(BF16), "w_out": w_out[l].astype(BF16),
        "w_up": w_up[l].astype(BF16), "w_down": w_down[l].astype(BF16),
    }


def _lambda_init(l):
    a, b, c = LAMBDA_INIT_BASE
    return a - b * math.exp(-c * l)


def _block(x, mod, layer, lw, rope_tabs, caches, tm_in, tm_post, tq):
    latent = rope_tabs is not None
    b_sz, n, d = x.shape

    def flat(a):
        return a if latent else a.reshape(1, b_sz * n, a.shape[-1])

    outs = [o.reshape(b_sz, n, o.shape[-1]) for o in _inproj(flat(x), mod, layer, lw, rope_tabs, tm_in)]
    dq, dk, dv, rq, rk, rv, rg, mq, mk, mv, gates = outs[:11]
    lam_init = _lambda_init(layer)
    diff_cache = caches["diff"] if latent else None
    mla_cache = caches["mla"][layer] if latent else None
    oa = _attention(dq, dk, dv, diff_cache, layer, diff=True, lam_init=lam_init, lmb=lw["lmb"],
                    subln=lw["subln"], tq=tq)
    om = _attention(mq, mk, mv, mla_cache, 0, diff=False, tq=4 * tq)
    ret = _retention(rq, rk, rv, rg, lw["ret_decay"], lw["ret_gn"], caches["state"] if latent else None, layer)
    x = _post(flat(x), flat(oa), flat(ret[0]), flat(om), flat(gates), mod, layer, lw, latent,
              tm_post).reshape(b_sz, n, d)
    if latent:
        return x, None
    return x, (outs[11], outs[12], outs[13], outs[14], ret[1])


def kernel(x_prompt, x_sample, cache_diff_k, cache_diff_v, cache_mla_ckv, cache_mla_krope, state_ret,
           c, c_ctx, w_mod, b_mod, norm1, norm2, w_in, diff_qn, diff_kn, diff_lambda, diff_subln,
           ret_decay, ret_gn, mla_qa_norm, w_mla_qb, mla_kva_norm, w_mla_kvb, mla_qn, mla_kn,
           w_branch, w_out, w_up, w_down):
    n_layers = w_in.shape[0]
    d = x_prompt.shape[-1]
    bd, n_lat = x_sample.shape[:2]
    b_ctx, n_ctx = x_prompt.shape[:2]
    p = cache_diff_k.shape[2]
    assert bd + 1 <= MOD_ROWS

    lws = [_layer_weights(l, w_in, norm1, norm2, diff_qn, diff_kn, diff_lambda, diff_subln, ret_decay, ret_gn,
                          mla_qa_norm, w_mla_qb, mla_kva_norm, w_mla_kvb, mla_qn, mla_kn, w_branch, w_out,
                          w_up, w_down) for l in range(n_layers)]

    cvec = jnp.concatenate([c_ctx[None, :], c, jnp.zeros((MOD_ROWS - 1 - bd, d), F32)], axis=0)
    mod = _modulation(cvec, w_mod, b_mod).reshape(n_layers, MOD_ROWS, 6, d)

    xp = x_prompt
    ctx_outs = []
    for l in range(n_layers):
        xp, ctx = _block(xp, mod, l, lws[l], None, None, 512, 512, 256)
        ctx_outs.append(ctx)
    new_diff_k = jnp.stack([o[0] for o in ctx_outs], axis=1).reshape(b_ctx, n_layers, n_ctx, 2 * DIFF_HEADS, DIFF_DH)
    new_diff_v = jnp.stack([o[1] for o in ctx_outs], axis=1).reshape(b_ctx, n_layers, n_ctx, DIFF_HEADS, 2 * DIFF_DH)
    new_mla_ckv = jnp.stack([o[2] for o in ctx_outs], axis=1)
    new_mla_krope = jnp.stack([o[3] for o in ctx_outs], axis=1)
    new_state_ret = jnp.stack([o[4] for o in ctx_outs], axis=1)

    caches = {
        "diff": (cache_diff_k.reshape(bd, n_layers, p, 2 * DIFF_HEADS * DIFF_DH),
                 cache_diff_v.reshape(bd, n_layers, p, DIFF_HEADS * 2 * DIFF_DH)),
        "mla": _context_keys(cache_mla_ckv, jnp.pad(cache_mla_krope, ((0, 0),) * 3 + ((0, LANES - MLA_ROPE),)), lws),
        "state": state_ret,
    }
    rope_tabs = _rope_tables(n_lat)
    xs = x_sample
    for l in range(n_layers):
        xs, _ = _block(xs, mod, l, lws[l], rope_tabs, caches, 512, 512, 256)

    return (xp, xs, new_diff_k, new_diff_v, new_mla_ckv, new_mla_krope, new_state_ret)
```

```python
import functools
import math

import jax
import jax.numpy as jnp
from jax import lax
from jax.experimental import pallas as pl
from jax.experimental.pallas import tpu as pltpu

F32 = jnp.float32
BF16 = jnp.bfloat16

D_MODEL = 1024
GRID_W = 64
DIFF_HEADS = 4
DIFF_DH = 64
RET_HEADS = 4
RET_DK = 64
RET_DV = 128
RET_CHUNK = 128
MLA_HEADS = 4
MLA_Q_LORA = 384
MLA_KV_LORA = 256
MLA_NOPE = 128
MLA_ROPE = 64
MLA_V = 128
MLA_QK = MLA_NOPE + MLA_ROPE
N_BRANCH = 3
BRANCH_W = 512
D_FF = 4 * D_MODEL
ROT_DIM = 64
ROPE_BASE = 10000.0
EPS = 1e-6
LAMBDA_INIT_BASE = (0.8, 0.6, 0.3)

LANES = 128
MXU_W = 256
MLA_PAD = 2 * LANES
HEAD_W = 128
LOG2E = 1.4426950408889634
VMEM_LIMIT = 56 * 1024 * 1024
TOKEN_TILE = 512
Q_TILE_DIFF = 256
Q_TILE_MLA = 1024
KEY_CHUNK = 512
RET_UNROLL = 16
_IDLE = object()
MOD_ROWS = 16


def _resident(shape):
    nd = len(shape)
    return pl.BlockSpec(shape, lambda *_: (0,) * nd, pipeline_mode=pl.Buffered(1))


def _params(n_grid):
    return pltpu.CompilerParams(dimension_semantics=("arbitrary",) * n_grid,
                                vmem_limit_bytes=VMEM_LIMIT)


def _sigmoid(x):
    return 1.0 / (1.0 + jnp.exp(-x))


def _rms(x, width):
    ss = jnp.sum(x * x, axis=-1, keepdims=True)
    return x * lax.rsqrt(ss * (1.0 / width) + EPS)


def _mod_kernel(c_ref, w_ref, b_ref, o_ref):
    c = c_ref[...]
    a = (c * _sigmoid(c)).astype(BF16)
    part = jnp.dot(a, w_ref[0].astype(BF16), preferred_element_type=F32)

    @pl.when(pl.program_id(1) == 0)
    def _():
        o_ref[0] = part + b_ref[0]

    @pl.when(pl.program_id(1) > 0)
    def _():
        o_ref[0] += part


def _modulation(cvec, w_mod, b_mod):
    n_layers, d, d6 = w_mod.shape
    tk = d // 4
    return pl.pallas_call(
        _mod_kernel,
        grid=(n_layers, d // tk),
        in_specs=[
            pl.BlockSpec((MOD_ROWS, tk), lambda l, k: (0, k)),
            pl.BlockSpec((1, tk, d6), lambda l, k: (l, k, 0)),
            pl.BlockSpec((1, 1, d6), lambda l, k: (l, 0, 0)),
        ],
        out_specs=pl.BlockSpec((1, MOD_ROWS, d6), lambda l, k: (l, 0, 0)),
        out_shape=jax.ShapeDtypeStruct((n_layers, MOD_ROWS, d6), F32),
        compiler_params=_params(2),
        name="modulation",
    )(cvec, w_mod, b_mod.reshape(n_layers, 1, d6))


def _inproj_kernel(*refs, latent):
    it = iter(refs)
    x_ref, mod_ref, n1_ref = next(it), next(it), next(it)
    if latent:
        cos_ref, sa_ref, sb_ref = next(it), next(it), next(it)
    w_diff, w_ret, w_mla, w_gate, w_qb, w_kn, w_v = (next(it) for _ in range(7))
    dqn_ref, dkn_ref, qag_ref, kvag_ref, mqn_ref, mkn_ref = (next(it) for _ in range(6))
    (dq_ref, dk_ref, dv_ref, rq_ref, rk_ref, rv_ref, rg_ref,
     mq_ref, mk_ref, mv_ref, gate_ref) = (next(it) for _ in range(11))
    if not latent:
        odk_ref, odv_ref, ockv_ref, okr_ref = (next(it) for _ in range(4))

    x = x_ref[0]
    mod = mod_ref[0, 0]
    h = _rms(x, D_MODEL) * (n1_ref[...] * (1.0 + mod[1:2])) + mod[0:1]
    hb = h.astype(BF16)

    if latent:
        cos, sa, sb = cos_ref[...], sa_ref[...], sb_ref[...]

        def rope(y):
            return (y * cos + pltpu.roll(y, LANES - 16, 1) * sa + pltpu.roll(y, 16, 1) * sb)
    else:
        def rope(y):
            return y

    def group(z, g):
        return z[:, g * LANES:(g + 1) * LANES]

    r_i = lax.broadcasted_iota(jnp.int32, (MXU_W, MXU_W), 0) // DIFF_DH
    c_i = lax.broadcasted_iota(jnp.int32, (MXU_W, MXU_W), 1) // DIFF_DH
    seg = jnp.where(r_i == c_i, 1.0, 0.0).astype(BF16)

    def qk_norm(y, gain):
        ss = jnp.dot((y * y).astype(BF16), seg, preferred_element_type=F32)
        return y * lax.rsqrt(ss * (1.0 / DIFF_DH) + EPS) * gain

    zd = jnp.dot(hb, w_diff[...], preferred_element_type=F32)
    n_g = BRANCH_W // LANES
    per_tile = MXU_W // LANES
    q_gain = dqn_ref[...] * (DIFF_DH ** -0.5 * LOG2E)
    for g2 in range(BRANCH_W // MXU_W):
        qn = qk_norm(zd[:, g2 * MXU_W:(g2 + 1) * MXU_W], q_gain)
        kn = qk_norm(zd[:, BRANCH_W + g2 * MXU_W:BRANCH_W + (g2 + 1) * MXU_W], dkn_ref[...])
        for g in range(g2 * per_tile, (g2 + 1) * per_tile):
            sl = slice(g * LANES, (g + 1) * LANES)
            dq_ref[0, :, sl] = rope(group(qn, g % per_tile)).astype(BF16)
            dk_ref[0, :, sl] = rope(group(kn, g % per_tile)).astype(BF16)
            v = group(zd, 2 * n_g + g)
            dv_ref[0, :, sl] = v.astype(BF16)
            if not latent:
                odk_ref[0, :, sl] = group(kn, g % per_tile)
                odv_ref[0, :, sl] = v

    zr = jnp.dot(hb, w_ret[...], preferred_element_type=F32)
    low = lax.broadcasted_iota(jnp.int32, (x.shape[0], LANES), 1) < RET_DK

    def twice(y):
        swapped = pltpu.roll(y, RET_DK, 1)
        return jnp.where(low, y, swapped), jnp.where(low, swapped, y)

    n_p = RET_HEADS * RET_DK // LANES
    for p in range(n_p):
        q_pair = twice(rope(group(zr, p)))
        k_pair = twice(rope(group(zr, n_p + p)) * RET_DK ** -0.5)
        for i in range(2):
            sl = slice((2 * p + i) * LANES, (2 * p + i + 1) * LANES)
            rq_ref[0, :, sl] = q_pair[i].astype(BF16)
            rk_ref[0, :, sl] = k_pair[i].astype(BF16)
    for g in range(n_g):
        sl = slice(g * LANES, (g + 1) * LANES)
        rv_ref[0, :, sl] = group(zr, 2 * n_p + g).astype(BF16)
        rg_ref[0, :, sl] = group(zr, 2 * n_p + n_g + g).astype(BF16)

    zm = jnp.dot(hb, w_mla[...], preferred_element_type=F32)
    qa = _rms(zm[:, :MLA_Q_LORA], MLA_Q_LORA) * qag_ref[...]
    mqz = jnp.dot(qa.astype(BF16), w_qb[...], preferred_element_type=F32)
    ckv = _rms(zm[:, MLA_Q_LORA:MLA_Q_LORA + MLA_KV_LORA], MLA_KV_LORA) * kvag_ref[...]
    ckvb = ckv.astype(BF16)
    k_nope = jnp.dot(ckvb, w_kn[...], preferred_element_type=F32)
    mv_ref[0] = jnp.dot(ckvb, w_v[...], preferred_element_type=F32).astype(BF16)
    kr = zm[:, MLA_Q_LORA + MLA_KV_LORA:]
    ss_kr = jnp.sum(kr * kr, axis=-1, keepdims=True)
    if not latent:
        ockv_ref[0] = ckv
        okr_ref[0] = kr[:, :MLA_ROPE]
    q_gain = mqn_ref[...] * (MLA_QK ** -0.5 * LOG2E)
    k_gain = mkn_ref[...]
    for hd in range(MLA_HEADS):
        lo = slice(hd * MLA_PAD, hd * MLA_PAD + LANES)
        hi = slice(hd * MLA_PAD + LANES, (hd + 1) * MLA_PAD)
        qh = _rms(mqz[:, hd * MLA_PAD:(hd + 1) * MLA_PAD], MLA_QK) * q_gain
        mq_ref[0, :, lo] = qh[:, :LANES].astype(BF16)
        mq_ref[0, :, hi] = rope(qh[:, LANES:]).astype(BF16)
        kno = k_nope[:, hd * LANES:(hd + 1) * LANES]
        ss = jnp.sum(kno * kno, axis=-1, keepdims=True) + ss_kr
        rstd = lax.rsqrt(ss * (1.0 / MLA_QK) + EPS)
        mk_ref[0, :, lo] = (kno * rstd * k_gain[:, :LANES]).astype(BF16)
        mk_ref[0, :, hi] = rope(kr * rstd * k_gain[:, LANES:]).astype(BF16)

    gate_ref[0] = jnp.dot(hb, w_gate[...], preferred_element_type=F32).astype(BF16)


def _inproj(x, mod, layer, lw, rope_tabs):
    b_sz, n, d = x.shape
    latent = rope_tabs is not None
    tm = min(TOKEN_TILE, n)
    assert n % tm == 0
    grid = (b_sz, n // tm)

    def tok(width):
        return pl.BlockSpec((1, tm, width), lambda b, i: (b, i, 0))

    if latent:
        mod_spec = pl.BlockSpec((1, 1, 6, d), lambda b, i: (layer, 1 + b, 0, 0))
    else:
        mod_spec = pl.BlockSpec((1, 1, 6, d), lambda b, i: (layer, 0, 0, 0))
    in_specs = [tok(d), mod_spec, _resident((1, d))]
    args = [x, mod, lw["norm1"]]
    if latent:
        in_specs += [pl.BlockSpec((tm, LANES), lambda b, i: (i, 0))] * 3
        args += list(rope_tabs)
    weights = [lw["w_diff"], lw["w_ret"], lw["w_mla"], lw["w_gate"], lw["w_qb"], lw["w_kn"], lw["w_v"]]
    gains = [lw["dqn"], lw["dkn"], lw["qa_g"], lw["kva_g"], lw["mqn"], lw["mkn"]]
    for a in weights + gains:
        in_specs.append(_resident(a.shape))
        args.append(a)

    widths = [BRANCH_W] * 7 + [MLA_HEADS * MLA_PAD] * 2 + [MLA_HEADS * MLA_V, N_BRANCH * d]
    out_specs = [tok(w) for w in widths]
    out_shape = [jax.ShapeDtypeStruct((b_sz, n, w), BF16) for w in widths]
    if not latent:
        for w in (BRANCH_W, BRANCH_W, MLA_KV_LORA, MLA_ROPE):
            out_specs.append(tok(w))
            out_shape.append(jax.ShapeDtypeStruct((b_sz, n, w), F32))
    return pl.pallas_call(
        functools.partial(_inproj_kernel, latent=latent),
        grid=grid, in_specs=in_specs, out_specs=out_specs, out_shape=out_shape,
        compiler_params=_params(2),
        name="inproj_latent" if latent else "inproj_context",
    )(*args)


def _attn_kernel(*refs, diff, has_cache, n_new, n_past, tq, ck, hpb, lam_init):
    it = iter(refs)
    q_ref, kn_ref, vn_ref = next(it), next(it), next(it)
    if has_cache:
        kc_ref, vc_ref = next(it), next(it)
    if diff:
        lmb_ref, sub_ref = next(it), next(it)
    o_ref = next(it)
    s_bufs = (next(it), next(it))
    q_tiles = n_new // tq
    n_tiles = hpb * q_tiles
    dq = q_ref.shape[-1] // hpb

    def chunks_of(head):
        kl = slice(head * dq, (head + 1) * dq)
        vl = slice(head * HEAD_W, (head + 1) * HEAD_W)
        out = [(lambda j=j: kn_ref[0, j * ck:(j + 1) * ck, kl], lambda j=j: vn_ref[0, j * ck:(j + 1) * ck, vl])
               for j in range(n_new // ck)]
        if has_cache:
            out += [(lambda j=j: kc_ref[0, 0, j * ck:(j + 1) * ck, kl].astype(BF16),
                     lambda j=j: vc_ref[0, 0, j * ck:(j + 1) * ck, vl].astype(BF16))
                    for j in range(n_past // ck)]
        return out

    def place(t):
        if hpb == 1:
            return 0, pl.ds(pl.multiple_of(t * tq, tq), tq)
        return t // q_tiles, slice((t % q_tiles) * tq, (t % q_tiles + 1) * tq)

    rows = 2 * tq if diff else tq

    def cols(j):
        return slice(j * ck, (j + 1) * ck)


    def score_stage(t, slot, _, split):
        head, q_rows = place(t)
        q = q_ref[0, q_rows, head * dq:(head + 1) * dq]
        if diff:
            lane = lax.broadcasted_iota(jnp.int32, q.shape, 1)
            qf = q.astype(F32)
            q = jnp.concatenate([jnp.where(lane < DIFF_DH, qf, 0.0), jnp.where(lane >= DIFF_DH, qf, 0.0)],
                                axis=0).astype(BF16)
        m_acc = jnp.full((rows, LANES), -jnp.inf, F32)
        for j, (k_of, _) in enumerate(chunks_of(head)):
            s = lax.dot_general(q, k_of(), (((1,), (1,)), ((), ())), preferred_element_type=F32)
            s_bufs[slot][:, cols(j)] = s
            for u in range(ck // LANES):
                m_acc = jnp.maximum(m_acc, s[:, u * LANES:(u + 1) * LANES])
            yield
        return jnp.broadcast_to(jnp.max(m_acc, axis=-1, keepdims=True), (rows, LANES))

    def slab(j, u):
        return slice(j * ck + u * LANES, j * ck + (u + 1) * LANES)

    def exp_of(slot, j, m):
        return [jnp.exp2(s_bufs[slot][:, slab(j, u)] - m) for u in range(ck // LANES)]

    def softmax_value_stage(t, slot, m, split):
        head, q_rows = place(t)
        n_part = 2 if split else 1
        part = rows // n_part
        accs = [jnp.zeros((part, HEAD_W + LANES), F32)] * n_part
        ones = jnp.ones((ck, LANES), BF16)
        for j, (_, v_of) in enumerate(chunks_of(head)):
            e = jnp.concatenate([x.astype(BF16) for x in exp_of(slot, j, m)], axis=1)
            v1 = jnp.concatenate([v_of(), ones], axis=1)
            accs = [accs[i] + jnp.dot(e[i * part:(i + 1) * part], v1, preferred_element_type=F32)
                    for i in range(n_part)]
            yield
        acc = jnp.concatenate(accs, axis=0)
        o = acc[:, :HEAD_W] / acc[:, HEAD_W:]
        if diff:
            lmb = lmb_ref[...]
            lam = (jnp.exp(jnp.sum(lmb[0:1] * lmb[1:2], axis=-1, keepdims=True))
                   - jnp.exp(jnp.sum(lmb[2:3] * lmb[3:4], axis=-1, keepdims=True)) + lam_init)
            o = o[:tq] - lam * o[tq:]
            o = _rms(o, HEAD_W) * sub_ref[...] * (1.0 - lam_init)
        o_ref[0, q_rows, head * HEAD_W:(head + 1) * HEAD_W] = o.astype(BF16)

    stage_fns = (score_stage, softmax_value_stage)
    n_stage = len(stage_fns)
    n_slot = 2

    def step(u, u_static, inputs, split):
        gens = [(k, fn(u - k, (u_static - k) % n_slot, inputs[k], split))
                for k, fn in enumerate(stage_fns) if inputs[k] is not _IDLE]
        new = {}
        while gens:
            for item in list(gens):
                try:
                    next(item[1])
                except StopIteration as done:
                    new[item[0]] = done.value
                    gens.remove(item)
        return new

    def inputs_of(results, u):
        return [(results[k - 1] if k else None) if 0 <= u - k < n_tiles else _IDLE for k in range(n_stage)]

    n_steps = n_tiles + n_stage - 1
    steady = [u for u in range(n_steps) if _IDLE not in inputs_of([None] * n_stage, u)]
    n_iter = len(steady) // n_slot
    rolled = set(steady[:n_iter * n_slot]) if n_iter >= 2 else set()
    assert not rolled or hpb == 1

    results = [None] * n_stage
    u = 0
    while u < n_steps:
        if u in rolled:
            def body(i, res, u0=u):
                res = list(res) + [None]
                for r in range(n_slot):
                    new = step(u0 + i * n_slot + r, u0 + r, inputs_of(res, u0 + r), False)
                    res = [new[k] for k in range(n_stage)]
                return tuple(res[:-1])

            results = list(lax.fori_loop(0, n_iter, body, tuple(results[:-1]))) + [None]
            u += n_iter * n_slot
        else:
            new = step(u, u, inputs_of(results, u), bool(rolled) and u > max(rolled))
            results = [new.get(k) for k in range(n_stage)]
            u += 1


def _attention(q, k_new, v_new, cache, layer, *, diff, lam_init=0.0, lmb=None, subln=None):
    b_sz, n, _ = q.shape
    dq = LANES if diff else MLA_PAD
    n_heads = DIFF_HEADS if diff else MLA_HEADS
    tq = min(Q_TILE_DIFF if diff else Q_TILE_MLA, n)
    has_cache = cache is not None
    n_past = cache[0].shape[2] if has_cache else 0
    ck = min(KEY_CHUNK, n, n_past) if has_cache else min(KEY_CHUNK, n)
    assert n % tq == 0 and n % ck == 0 and n_past % ck == 0
    hpb = n_heads if n // tq == 1 else 1

    in_specs = [
        pl.BlockSpec((1, n, hpb * dq), lambda b, h: (b, 0, h)),
        pl.BlockSpec((1, n, hpb * dq), lambda b, h: (b, 0, h)),
        pl.BlockSpec((1, n, hpb * HEAD_W), lambda b, h: (b, 0, h)),
    ]
    args = [q, k_new, v_new]
    if has_cache:
        in_specs += [
            pl.BlockSpec((1, 1, n_past, hpb * dq), lambda b, h: (b, layer, 0, h)),
            pl.BlockSpec((1, 1, n_past, hpb * HEAD_W), lambda b, h: (b, layer, 0, h)),
        ]
        args += list(cache)
    if diff:
        in_specs += [_resident(lmb.shape), _resident(subln.shape)]
        args += [lmb, subln]
    rows = 2 * tq if diff else tq
    return pl.pallas_call(
        functools.partial(_attn_kernel, diff=diff, has_cache=has_cache, n_new=n, n_past=n_past,
                          tq=tq, ck=ck, hpb=hpb, lam_init=lam_init),
        grid=(b_sz, n_heads // hpb), in_specs=in_specs,
        out_specs=pl.BlockSpec((1, n, hpb * HEAD_W), lambda b, h: (b, 0, h)),
        out_shape=jax.ShapeDtypeStruct((b_sz, n, n_heads * HEAD_W), BF16),
        scratch_shapes=[pltpu.VMEM((rows, n + n_past), F32)] * 2,
        compiler_params=_params(2),
        name=("diff_attn" if diff else "mla_attn") + ("_latent" if has_cache else "_context"),
    )(*args)


def _log_sigmoid(x):
    return jnp.minimum(x, 0.0) - jnp.log(1.0 + jnp.exp(-jnp.abs(x)))


def _ret_kernel(*refs, latent, n_chunks, hpb):
    it = iter(refs)
    dec_ref, q_ref, k_ref, v_ref, g_ref, gn_ref = (next(it) for _ in range(6))
    if latent:
        r0_ref = next(it)
    o_ref = next(it)
    if not latent:
        st_ref = next(it)
    rcat_scr, u_scr = next(it), next(it)

    c_len = RET_CHUNK
    row = lax.broadcasted_iota(jnp.int32, (c_len, c_len), 0).astype(F32)
    col = lax.broadcasted_iota(jnp.int32, (c_len, c_len), 1).astype(F32)
    lane_fwd = lax.broadcasted_iota(jnp.int32, (c_len, 2 * RET_DK), 1) < RET_DK
    gn = gn_ref[...]
    unroll = min(RET_UNROLL, n_chunks)

    def chunk(c):
        return pl.ds(pl.multiple_of(c * c_len, c_len), c_len)

    def one_head(hh):
        hd = pl.program_id(1) * hpb + hh
        lanes = slice(hh * HEAD_W, (hh + 1) * HEAD_W)
        lg_f = _log_sigmoid(jnp.full((1, 1), dec_ref[0, hd], F32))
        lg_b = _log_sigmoid(jnp.full((1, 1), dec_ref[1, hd], F32))
        d_in = 0.5 * jnp.where(row >= col, jnp.exp((row - col) * lg_f), jnp.exp((col - row) * lg_b))
        q_dec = jnp.where(lane_fwd, jnp.exp((row + 1.0) * lg_f), jnp.exp((c_len - row) * lg_b))
        k_dec = jnp.where(lane_fwd, jnp.exp((c_len - 1.0 - row) * lg_f), jnp.exp(row * lg_b))
        dec_f = jnp.exp(c_len * lg_f)
        dec_b = jnp.exp(c_len * lg_b)

        def inc_body(c, carry):
            kd = (k_ref[0, chunk(c), lanes].astype(F32) * k_dec).astype(BF16)
            u_scr[hh, c] = lax.dot_general(kd, v_ref[0, chunk(c), lanes], (((0,), (0,)), ((), ())),
                                           preferred_element_type=F32)
            return carry

        lax.fori_loop(0, n_chunks, inc_body, 0, unroll=unroll)

        if latent:
            r_f0 = r0_ref[0, 0, 0, hh]
            r_b0 = r0_ref[0, 0, 1, hh]
        else:
            r_f0 = jnp.zeros((RET_DK, RET_DV), F32)
            r_b0 = r_f0

        def fwd_body(c, r):
            rcat_scr[hh, c, 0:RET_DK, :] = r.astype(BF16)
            return dec_f * r + u_scr[hh, c, 0:RET_DK, :]

        def bwd_body(t, r):
            c = n_chunks - 1 - t
            rcat_scr[hh, c, RET_DK:2 * RET_DK, :] = r.astype(BF16)
            return dec_b * r + u_scr[hh, c, RET_DK:2 * RET_DK, :]

        r_f = lax.fori_loop(0, n_chunks, fwd_body, r_f0, unroll=n_chunks <= 4)
        r_b = lax.fori_loop(0, n_chunks, bwd_body, r_b0, unroll=n_chunks <= 4)
        if not latent:
            st_ref[0, 0, hh] = r_f
            st_ref[0, 1, hh] = r_b

        def out_body(c, carry):
            qc = q_ref[0, chunk(c), lanes]
            kc = k_ref[0, chunk(c), lanes]
            vc = v_ref[0, chunk(c), lanes]
            a = lax.dot_general(qc, kc, (((1,), (1,)), ((), ())), preferred_element_type=F32) * d_in
            qd = (qc.astype(F32) * q_dec).astype(BF16)
            o = (jnp.dot(a.astype(BF16), vc, preferred_element_type=F32)
                 + jnp.dot(qd, rcat_scr[hh, c], preferred_element_type=F32))
            gate = g_ref[0, chunk(c), lanes].astype(F32)
            y = _rms(o, RET_DV) * gn * (gate * _sigmoid(gate))
            o_ref[0, chunk(c), lanes] = y.astype(BF16)
            return carry

        lax.fori_loop(0, n_chunks, out_body, 0, unroll=unroll)

    for hh in range(hpb):
        one_head(hh)


def _retention(rq, rk, rv, rg, ret_decay, ret_gn, state, layer):
    b_sz, n, _ = rq.shape
    latent = state is not None
    n_chunks = n // RET_CHUNK
    hpb = RET_HEADS if n_chunks <= 4 else 1
    tokw = pl.BlockSpec((1, n, hpb * HEAD_W), lambda b, h: (b, 0, h))
    in_specs = [pl.BlockSpec(memory_space=pltpu.SMEM), tokw, tokw, tokw, tokw, _resident(ret_gn.shape)]
    args = [ret_decay, rq, rk, rv, rg, ret_gn]
    if latent:
        in_specs.append(pl.BlockSpec((1, 1, 2, hpb, RET_DK, RET_DV), lambda b, h: (b, layer, 0, h, 0, 0)))
        args.append(state)
    out_specs = [tokw]
    out_shape = [jax.ShapeDtypeStruct((b_sz, n, RET_HEADS * HEAD_W), BF16)]
    if not latent:
        out_specs.append(pl.BlockSpec((1, 2, hpb, RET_DK, RET_DV), lambda b, h: (b, 0, h, 0, 0)))
        out_shape.append(jax.ShapeDtypeStruct((b_sz, 2, RET_HEADS, RET_DK, RET_DV), F32))
    return pl.pallas_call(
        functools.partial(_ret_kernel, latent=latent, n_chunks=n_chunks, hpb=hpb),
        grid=(b_sz, RET_HEADS // hpb), in_specs=in_specs, out_specs=out_specs, out_shape=out_shape,
        scratch_shapes=[pltpu.VMEM((hpb, n_chunks, RET_CHUNK, RET_DV), BF16),
                        pltpu.VMEM((hpb, n_chunks, RET_CHUNK, RET_DV), F32)],
        compiler_params=_params(2),
        name="retention_latent" if latent else "retention_context",
    )(*args)


def _ctxkeys_kernel(ckv_ref, kr_ref, w_kn, w_v, mkn_ref, mk_ref, mv_ref):
    ckvb = ckv_ref[0, 0].astype(BF16)
    kr = kr_ref[0, 0]
    k_nope = jnp.dot(ckvb, w_kn[...], preferred_element_type=F32)
    mv_ref[0, 0] = jnp.dot(ckvb, w_v[...], preferred_element_type=F32).astype(BF16)
    ss_kr = jnp.sum(kr * kr, axis=-1, keepdims=True)
    k_gain = mkn_ref[...]
    for hd in range(MLA_HEADS):
        kno = k_nope[:, hd * LANES:(hd + 1) * LANES]
        ss = jnp.sum(kno * kno, axis=-1, keepdims=True) + ss_kr
        rstd = lax.rsqrt(ss * (1.0 / MLA_QK) + EPS)
        mk_ref[0, 0, :, hd * MLA_PAD:hd * MLA_PAD + LANES] = (kno * rstd * k_gain[:, :LANES]).astype(BF16)
        mk_ref[0, 0, :, hd * MLA_PAD + LANES:(hd + 1) * MLA_PAD] = (kr * rstd * k_gain[:, LANES:]).astype(BF16)


def _context_keys(cache_ckv, cache_kr_pad, lws):
    b_sz, n_layers, p, _ = cache_ckv.shape
    outs = []
    for layer, lw in enumerate(lws):
        outs.append(pl.pallas_call(
            _ctxkeys_kernel,
            grid=(b_sz,),
            in_specs=[
                pl.BlockSpec((1, 1, p, MLA_KV_LORA), lambda b, layer=layer: (b, layer, 0, 0)),
                pl.BlockSpec((1, 1, p, LANES), lambda b, layer=layer: (b, layer, 0, 0)),
                _resident(lw["w_kn"].shape), _resident(lw["w_v"].shape), _resident(lw["mkn"].shape),
            ],
            out_specs=[pl.BlockSpec((1, 1, p, MLA_HEADS * MLA_PAD), lambda b: (b, 0, 0, 0)),
                       pl.BlockSpec((1, 1, p, MLA_HEADS * MLA_V), lambda b: (b, 0, 0, 0))],
            out_shape=[jax.ShapeDtypeStruct((b_sz, 1, p, MLA_HEADS * MLA_PAD), BF16),
                       jax.ShapeDtypeStruct((b_sz, 1, p, MLA_HEADS * MLA_V), BF16)],
            compiler_params=_params(1),
            name="mla_context_keys",
        )(cache_ckv, cache_kr_pad, lw["w_kn"], lw["w_v"], lw["mkn"]))
    return outs


def _post_kernel(x_ref, oa_ref, or_ref, om_ref, g_ref, mod_ref, n2_ref, wb_ref, wo_ref, wu_ref, wd_ref,
                 o_ref, *, ff_chunk):
    x = x_ref[0]
    mod = mod_ref[0, 0]
    d = x.shape[-1]
    merged = jnp.zeros(x.shape, F32)
    for i, br_ref in enumerate((oa_ref, or_ref, om_ref)):
        gate = _sigmoid(g_ref[0, :, i * d:(i + 1) * d].astype(F32))
        merged = merged + gate * jnp.dot(br_ref[0], wb_ref[i], preferred_element_type=F32)
    mix = jnp.dot(merged.astype(BF16), wo_ref[...], preferred_element_type=F32)
    x1 = x + mod[2:3] * mix
    h2 = (_rms(x1, d) * (n2_ref[...] * (1.0 + mod[4:5])) + mod[3:4]).astype(BF16)
    acc = jnp.zeros(x.shape, F32)
    for c in range(D_FF // ff_chunk):
        u = jnp.dot(h2, wu_ref[:, c * ff_chunk:(c + 1) * ff_chunk], preferred_element_type=F32)
        u = jnp.square(jnp.maximum(u, 0.0)).astype(BF16)
        acc = acc + jnp.dot(u, wd_ref[c * ff_chunk:(c + 1) * ff_chunk, :], preferred_element_type=F32)
    o_ref[0] = x1 + mod[5:6] * acc


def _post(x, oa, orr, om, gates, mod, layer, lw, latent):
    b_sz, n, d = x.shape
    tm = min(TOKEN_TILE, n)
    assert n % tm == 0

    def tok(width):
        return pl.BlockSpec((1, tm, width), lambda b, i: (b, i, 0))

    if latent:
        mod_spec = pl.BlockSpec((1, 1, 6, d), lambda b, i: (layer, 1 + b, 0, 0))
    else:
        mod_spec = pl.BlockSpec((1, 1, 6, d), lambda b, i: (layer, 0, 0, 0))
    weights = [lw["norm2"], lw["w_branch"], lw["w_out"], lw["w_up"], lw["w_down"]]
    return pl.pallas_call(
        functools.partial(_post_kernel, ff_chunk=1024),
        grid=(b_sz, n // tm),
        in_specs=[tok(d), tok(BRANCH_W), tok(BRANCH_W), tok(BRANCH_W), tok(N_BRANCH * d), mod_spec]
                 + [_resident(w.shape) for w in weights],
        out_specs=tok(d),
        out_shape=jax.ShapeDtypeStruct((b_sz, n, d), F32),
        compiler_params=_params(2),
        name="merge_mlp_latent" if latent else "merge_mlp_context",
    )(x, oa, orr, om, gates, mod, *weights)


def _rope_tables(n):
    pos = jnp.arange(n, dtype=jnp.int32)
    row = (pos // GRID_W).astype(F32)
    col = (pos % GRID_W).astype(F32)
    axis_dim = ROT_DIM // 2
    inv = ROPE_BASE ** (-jnp.arange(0, axis_dim, 2, dtype=F32) / axis_dim)
    ang_r = row[:, None] * inv[None, :]
    ang_c = col[:, None] * inv[None, :]
    zeros = jnp.zeros_like(ang_r)
    cos = jnp.concatenate([jnp.cos(ang_r)] * 2 + [jnp.cos(ang_c)] * 2, axis=-1)
    sin_a = jnp.concatenate([-jnp.sin(ang_r), zeros, -jnp.sin(ang_c), zeros], axis=-1)
    sin_b = jnp.concatenate([zeros, jnp.sin(ang_r), zeros, jnp.sin(ang_c)], axis=-1)
    reps = LANES // ROT_DIM
    return tuple(jnp.tile(t, (1, reps)) for t in (cos, sin_a, sin_b))


def _layer_weights(l, w_in, norm1, norm2, diff_qn, diff_kn, diff_lambda, diff_subln, ret_decay, ret_gn,
                   mla_qa_norm, w_mla_qb, mla_kva_norm, w_mla_kvb, mla_qn, mla_kn, w_branch, w_out, w_up,
                   w_down):
    d = w_in.shape[1]
    w = w_in[l]
    o_ret = 3 * BRANCH_W
    o_rk = o_ret + RET_HEADS * RET_DK
    o_rv = o_rk + RET_HEADS * RET_DK
    o_rg = o_rv + RET_HEADS * RET_DV
    o_qa = o_rg + RET_HEADS * RET_DV
    o_kva = o_qa + MLA_Q_LORA
    o_kr = o_kva + MLA_KV_LORA
    o_gate = o_kr + MLA_ROPE

    w_mla = jnp.concatenate([w[:, o_qa:o_gate], jnp.zeros((d, LANES - MLA_ROPE), w.dtype)], axis=-1)
    w_qb = jnp.pad(w_mla_qb[l].reshape(MLA_Q_LORA, MLA_HEADS, MLA_QK), ((0, 0), (0, 0), (0, MLA_PAD - MLA_QK)))
    w_kvb = w_mla_kvb[l].reshape(MLA_KV_LORA, MLA_HEADS, MLA_NOPE + MLA_V)

    def pad_gain(g):
        return jnp.pad(g, (0, MLA_PAD - MLA_QK))[None, :]

    return {
        "w_diff": w[:, :o_ret].astype(BF16),
        "w_ret": w[:, o_ret:o_qa].astype(BF16),
        "w_mla": w_mla.astype(BF16),
        "w_gate": w[:, o_gate:].astype(BF16),
        "w_qb": w_qb.reshape(MLA_Q_LORA, MLA_HEADS * MLA_PAD).astype(BF16),
        "w_kn": w_kvb[:, :, :MLA_NOPE].reshape(MLA_KV_LORA, MLA_HEADS * MLA_NOPE).astype(BF16),
        "w_v": w_kvb[:, :, MLA_NOPE:].reshape(MLA_KV_LORA, MLA_HEADS * MLA_V).astype(BF16),
        "norm1": norm1[l][None, :], "norm2": norm2[l][None, :],
        "dqn": jnp.tile(diff_qn[l], MXU_W // DIFF_DH)[None, :],
        "dkn": jnp.tile(diff_kn[l], MXU_W // DIFF_DH)[None, :],
        "qa_g": mla_qa_norm[l][None, :], "kva_g": mla_kva_norm[l][None, :],
        "mqn": pad_gain(mla_qn[l]), "mkn": pad_gain(mla_kn[l]),
        "lmb": diff_lambda[l], "subln": diff_subln[l][None, :],
        "ret_decay": ret_decay[l], "ret_gn": ret_gn[l][None, :],
        "w_branch": w_branch[l].astype(BF16), "w_out": w_out[l].astype(BF16),
        "w_up": w_up[l].astype(BF16), "w_down": w_down[l].astype(BF16),
    }


def _lambda_init(l):
    a, b, c = LAMBDA_INIT_BASE
    return a - b * math.exp(-c * l)


def _block(x, mod, layer, lw, rope_tabs, caches):
    latent = rope_tabs is not None
    b_sz, n, d = x.shape

    def flat(a):
        return a if latent else a.reshape(1, b_sz * n, a.shape[-1])

    outs = [o.reshape(b_sz, n, o.shape[-1]) for o in _inproj(flat(x), mod, layer, lw, rope_tabs)]
    dq, dk, dv, rq, rk, rv, rg, mq, mk, mv, gates = outs[:11]
    lam_init = _lambda_init(layer)
    diff_cache = caches["diff"] if latent else None
    mla_cache = caches["mla"][layer] if latent else None
    oa = _attention(dq, dk, dv, diff_cache, layer, diff=True, lam_init=lam_init, lmb=lw["lmb"],
                    subln=lw["subln"])
    om = _attention(mq, mk, mv, mla_cache, 0, diff=False)
    ret = _retention(rq, rk, rv, rg, lw["ret_decay"], lw["ret_gn"], caches["state"] if latent else None, layer)
    x = _post(flat(x), flat(oa), flat(ret[0]), flat(om), flat(gates), mod, layer, lw, latent).reshape(b_sz, n, d)
    if latent:
        return x, None
    return x, (outs[11], outs[12], outs[13], outs[14], ret[1])


def kernel(x_prompt, x_sample, cache_diff_k, cache_diff_v, cache_mla_ckv, cache_mla_krope, state_ret,
           c, c_ctx, w_mod, b_mod, norm1, norm2, w_in, diff_qn, diff_kn, diff_lambda, diff_subln,
           ret_decay, ret_gn, mla_qa_norm, w_mla_qb, mla_kva_norm, w_mla_kvb, mla_qn, mla_kn,
           w_branch, w_out, w_up, w_down):
    n_layers = w_in.shape[0]
    d = x_prompt.shape[-1]
    bd, n_lat = x_sample.shape[:2]
    b_ctx, n_ctx = x_prompt.shape[:2]
    p = cache_diff_k.shape[2]
    assert bd + 1 <= MOD_ROWS

    lws = [_layer_weights(l, w_in, norm1, norm2, diff_qn, diff_kn, diff_lambda, diff_subln, ret_decay, ret_gn,
                          mla_qa_norm, w_mla_qb, mla_kva_norm, w_mla_kvb, mla_qn, mla_kn, w_branch, w_out,
                          w_up, w_down) for l in range(n_layers)]

    cvec = jnp.concatenate([c_ctx[None, :], c, jnp.zeros((MOD_ROWS - 1 - bd, d), F32)], axis=0)
    mod = _modulation(cvec, w_mod, b_mod).reshape(n_layers, MOD_ROWS, 6, d)

    xp = x_prompt
    ctx_outs = []
    for l in range(n_layers):
        xp, ctx = _block(xp, mod, l, lws[l], None, None)
        ctx_outs.append(ctx)
    new_diff_k = jnp.stack([o[0] for o in ctx_outs], axis=1).reshape(b_ctx, n_layers, n_ctx, 2 * DIFF_HEADS, DIFF_DH)
    new_diff_v = jnp.stack([o[1] for o in ctx_outs], axis=1).reshape(b_ctx, n_layers, n_ctx, DIFF_HEADS, 2 * DIFF_DH)
    new_mla_ckv = jnp.stack([o[2] for o in ctx_outs], axis=1)
    new_mla_krope = jnp.stack([o[3] for o in ctx_outs], axis=1)
    new_state_ret = jnp.stack([o[4] for o in ctx_outs], axis=1)

    caches = {
        "diff": (cache_diff_k.reshape(bd, n_layers, p, 2 * DIFF_HEADS * DIFF_DH),
                 cache_diff_v.reshape(bd, n_layers, p, DIFF_HEADS * 2 * DIFF_DH)),
        "mla": _context_keys(cache_mla_ckv, jnp.pad(cache_mla_krope, ((0, 0),) * 3 + ((0, LANES - MLA_ROPE),)), lws),
        "state": state_ret,
    }
    rope_tabs = _rope_tables(n_lat)
    xs = x_sample
    for l in range(n_layers):
        xs, _ = _block(xs, mod, l, lws[l], rope_tabs, caches)

    return (xp, xs, new_diff_k, new_diff_v, new_mla_ckv, new_mla_krope, new_state_ret)
```

```python
import functools
import math

import jax
import jax.numpy as jnp
from jax import lax
from jax.experimental import pallas as pl
from jax.experimental.pallas import tpu as pltpu

F32 = jnp.float32
BF16 = jnp.bfloat16

D_MODEL = 1024
GRID_W = 64
DIFF_HEADS = 4
DIFF_DH = 64
RET_HEADS = 4
RET_DK = 64
RET_DV = 128
RET_CHUNK = 128
MLA_HEADS = 4
MLA_Q_LORA = 384
MLA_KV_LORA = 256
MLA_NOPE = 128
MLA_ROPE = 64
MLA_V = 128
MLA_QK = MLA_NOPE + MLA_ROPE
N_BRANCH = 3
BRANCH_W = 512
D_FF = 4 * D_MODEL
ROT_DIM = 64
ROPE_BASE = 10000.0
EPS = 1e-6
LAMBDA_INIT_BASE = (0.8, 0.6, 0.3)

LANES = 128
MXU_W = 256
MLA_PAD = 2 * LANES
HEAD_W = 128
LOG2E = 1.4426950408889634
VMEM_LIMIT = 56 * 1024 * 1024
TOKEN_TILE = 512
Q_TILE_DIFF = 512
Q_TILE_MLA = 1024
KEY_CHUNK = 512
RET_UNROLL = 16
_IDLE = object()
MOD_ROWS = 16


def _resident(shape):
    nd = len(shape)
    return pl.BlockSpec(shape, lambda *_: (0,) * nd, pipeline_mode=pl.Buffered(1))


def _params(n_grid):
    return pltpu.CompilerParams(dimension_semantics=("arbitrary",) * n_grid,
                                vmem_limit_bytes=VMEM_LIMIT)


def _sigmoid(x):
    return 1.0 / (1.0 + jnp.exp(-x))


def _rms(x, width):
    ss = jnp.sum(x * x, axis=-1, keepdims=True)
    return x * lax.rsqrt(ss * (1.0 / width) + EPS)


def _mod_kernel(c_ref, w_ref, b_ref, o_ref):
    c = c_ref[...]
    a = (c * _sigmoid(c)).astype(BF16)
    part = jnp.dot(a, w_ref[0].astype(BF16), preferred_element_type=F32)

    @pl.when(pl.program_id(1) == 0)
    def _():
        o_ref[0] = part + b_ref[0]

    @pl.when(pl.program_id(1) > 0)
    def _():
        o_ref[0] += part


def _modulation(cvec, w_mod, b_mod):
    n_layers, d, d6 = w_mod.shape
    tk = d // 4
    return pl.pallas_call(
        _mod_kernel,
        grid=(n_layers, d // tk),
        in_specs=[
            pl.BlockSpec((MOD_ROWS, tk), lambda l, k: (0, k)),
            pl.BlockSpec((1, tk, d6), lambda l, k: (l, k, 0)),
            pl.BlockSpec((1, 1, d6), lambda l, k: (l, 0, 0)),
        ],
        out_specs=pl.BlockSpec((1, MOD_ROWS, d6), lambda l, k: (l, 0, 0)),
        out_shape=jax.ShapeDtypeStruct((n_layers, MOD_ROWS, d6), F32),
        compiler_params=_params(2),
        name="modulation",
    )(cvec, w_mod, b_mod.reshape(n_layers, 1, d6))


def _inproj_kernel(*refs, latent):
    it = iter(refs)
    x_ref, mod_ref, n1_ref = next(it), next(it), next(it)
    if latent:
        cos_ref, sa_ref, sb_ref = next(it), next(it), next(it)
    w_diff, w_ret, w_mla, w_gate, w_qb, w_kn, w_v = (next(it) for _ in range(7))
    dqn_ref, dkn_ref, qag_ref, kvag_ref, mqn_ref, mkn_ref = (next(it) for _ in range(6))
    (dq_ref, dk_ref, dv_ref, rq_ref, rk_ref, rv_ref, rg_ref,
     mq_ref, mk_ref, mv_ref, gate_ref) = (next(it) for _ in range(11))
    if not latent:
        odk_ref, odv_ref, ockv_ref, okr_ref = (next(it) for _ in range(4))

    x = x_ref[0]
    mod = mod_ref[0, 0]
    h = _rms(x, D_MODEL) * (n1_ref[...] * (1.0 + mod[1:2])) + mod[0:1]
    hb = h.astype(BF16)

    if latent:
        cos, sa, sb = cos_ref[...], sa_ref[...], sb_ref[...]

        def rope(y):
            return (y * cos + pltpu.roll(y, LANES - 16, 1) * sa + pltpu.roll(y, 16, 1) * sb)
    else:
        def rope(y):
            return y

    def group(z, g):
        return z[:, g * LANES:(g + 1) * LANES]

    r_i = lax.broadcasted_iota(jnp.int32, (MXU_W, MXU_W), 0) // DIFF_DH
    c_i = lax.broadcasted_iota(jnp.int32, (MXU_W, MXU_W), 1) // DIFF_DH
    seg = jnp.where(r_i == c_i, 1.0, 0.0).astype(BF16)

    def qk_norm(y, gain):
        ss = jnp.dot((y * y).astype(BF16), seg, preferred_element_type=F32)
        return y * lax.rsqrt(ss * (1.0 / DIFF_DH) + EPS) * gain

    zd = jnp.dot(hb, w_diff[...], preferred_element_type=F32)
    n_g = BRANCH_W // LANES
    per_tile = MXU_W // LANES
    q_gain = dqn_ref[...] * (DIFF_DH ** -0.5 * LOG2E)
    for g2 in range(BRANCH_W // MXU_W):
        qn = qk_norm(zd[:, g2 * MXU_W:(g2 + 1) * MXU_W], q_gain)
        kn = qk_norm(zd[:, BRANCH_W + g2 * MXU_W:BRANCH_W + (g2 + 1) * MXU_W], dkn_ref[...])
        for g in range(g2 * per_tile, (g2 + 1) * per_tile):
            sl = slice(g * LANES, (g + 1) * LANES)
            dq_ref[0, :, sl] = rope(group(qn, g % per_tile)).astype(BF16)
            dk_ref[0, :, sl] = rope(group(kn, g % per_tile)).astype(BF16)
            v = group(zd, 2 * n_g + g)
            dv_ref[0, :, sl] = v.astype(BF16)
            if not latent:
                odk_ref[0, :, sl] = group(kn, g % per_tile)
                odv_ref[0, :, sl] = v

    zr = jnp.dot(hb, w_ret[...], preferred_element_type=F32)
    low = lax.broadcasted_iota(jnp.int32, (x.shape[0], LANES), 1) < RET_DK

    def twice(y):
        swapped = pltpu.roll(y, RET_DK, 1)
        return jnp.where(low, y, swapped), jnp.where(low, swapped, y)

    n_p = RET_HEADS * RET_DK // LANES
    for p in range(n_p):
        q_pair = twice(rope(group(zr, p)))
        k_pair = twice(rope(group(zr, n_p + p)) * RET_DK ** -0.5)
        for i in range(2):
            sl = slice((2 * p + i) * LANES, (2 * p + i + 1) * LANES)
            rq_ref[0, :, sl] = q_pair[i].astype(BF16)
            rk_ref[0, :, sl] = k_pair[i].astype(BF16)
    for g in range(n_g):
        sl = slice(g * LANES, (g + 1) * LANES)
        rv_ref[0, :, sl] = group(zr, 2 * n_p + g).astype(BF16)
        rg_ref[0, :, sl] = group(zr, 2 * n_p + n_g + g).astype(BF16)

    zm = jnp.dot(hb, w_mla[...], preferred_element_type=F32)
    qa = _rms(zm[:, :MLA_Q_LORA], MLA_Q_LORA) * qag_ref[...]
    mqz = jnp.dot(qa.astype(BF16), w_qb[...], preferred_element_type=F32)
    ckv = _rms(zm[:, MLA_Q_LORA:MLA_Q_LORA + MLA_KV_LORA], MLA_KV_LORA) * kvag_ref[...]
    ckvb = ckv.astype(BF16)
    k_nope = jnp.dot(ckvb, w_kn[...], preferred_element_type=F32)
    mv_ref[0] = jnp.dot(ckvb, w_v[...], preferred_element_type=F32).astype(BF16)
    kr = zm[:, MLA_Q_LORA + MLA_KV_LORA:]
    ss_kr = jnp.sum(kr * kr, axis=-1, keepdims=True)
    if not latent:
        ockv_ref[0] = ckv
        okr_ref[0] = kr[:, :MLA_ROPE]
    q_gain = mqn_ref[...] * (MLA_QK ** -0.5 * LOG2E)
    k_gain = mkn_ref[...]
    for hd in range(MLA_HEADS):
        lo = slice(hd * MLA_PAD, hd * MLA_PAD + LANES)
        hi = slice(hd * MLA_PAD + LANES, (hd + 1) * MLA_PAD)
        qh = _rms(mqz[:, hd * MLA_PAD:(hd + 1) * MLA_PAD], MLA_QK) * q_gain
        mq_ref[0, :, lo] = qh[:, :LANES].astype(BF16)
        mq_ref[0, :, hi] = rope(qh[:, LANES:]).astype(BF16)
        kno = k_nope[:, hd * LANES:(hd + 1) * LANES]
        ss = jnp.sum(kno * kno, axis=-1, keepdims=True) + ss_kr
        rstd = lax.rsqrt(ss * (1.0 / MLA_QK) + EPS)
        mk_ref[0, :, lo] = (kno * rstd * k_gain[:, :LANES]).astype(BF16)
        mk_ref[0, :, hi] = rope(kr * rstd * k_gain[:, LANES:]).astype(BF16)

    gate_ref[0] = jnp.dot(hb, w_gate[...], preferred_element_type=F32).astype(BF16)


def _inproj(x, mod, layer, lw, rope_tabs):
    b_sz, n, d = x.shape
    latent = rope_tabs is not None
    tm = min(TOKEN_TILE, n)
    assert n % tm == 0
    grid = (b_sz, n // tm)

    def tok(width):
        return pl.BlockSpec((1, tm, width), lambda b, i: (b, i, 0))

    if latent:
        mod_spec = pl.BlockSpec((1, 1, 6, d), lambda b, i: (layer, 1 + b, 0, 0))
    else:
        mod_spec = pl.BlockSpec((1, 1, 6, d), lambda b, i: (layer, 0, 0, 0))
    in_specs = [tok(d), mod_spec, _resident((1, d))]
    args = [x, mod, lw["norm1"]]
    if latent:
        in_specs += [pl.BlockSpec((tm, LANES), lambda b, i: (i, 0))] * 3
        args += list(rope_tabs)
    weights = [lw["w_diff"], lw["w_ret"], lw["w_mla"], lw["w_gate"], lw["w_qb"], lw["w_kn"], lw["w_v"]]
    gains = [lw["dqn"], lw["dkn"], lw["qa_g"], lw["kva_g"], lw["mqn"], lw["mkn"]]
    for a in weights + gains:
        in_specs.append(_resident(a.shape))
        args.append(a)

    widths = [BRANCH_W] * 7 + [MLA_HEADS * MLA_PAD] * 2 + [MLA_HEADS * MLA_V, N_BRANCH * d]
    out_specs = [tok(w) for w in widths]
    out_shape = [jax.ShapeDtypeStruct((b_sz, n, w), BF16) for w in widths]
    if not latent:
        for w in (BRANCH_W, BRANCH_W, MLA_KV_LORA, MLA_ROPE):
            out_specs.append(tok(w))
            out_shape.append(jax.ShapeDtypeStruct((b_sz, n, w), F32))
    return pl.pallas_call(
        functools.partial(_inproj_kernel, latent=latent),
        grid=grid, in_specs=in_specs, out_specs=out_specs, out_shape=out_shape,
        compiler_params=_params(2),
        name="inproj_latent" if latent else "inproj_context",
    )(*args)


def _attn_kernel(*refs, diff, has_cache, n_new, n_past, tq, ck, hpb, lam_init):
    it = iter(refs)
    q_ref, kn_ref, vn_ref = next(it), next(it), next(it)
    if has_cache:
        kc_ref, vc_ref = next(it), next(it)
    if diff:
        lmb_ref, sub_ref = next(it), next(it)
    o_ref = next(it)
    s_bufs = (next(it), next(it))
    q_tiles = n_new // tq
    n_tiles = hpb * q_tiles
    dq = q_ref.shape[-1] // hpb

    def chunks_of(head):
        kl = slice(head * dq, (head + 1) * dq)
        vl = slice(head * HEAD_W, (head + 1) * HEAD_W)
        out = [(lambda j=j: kn_ref[0, j * ck:(j + 1) * ck, kl], lambda j=j: vn_ref[0, j * ck:(j + 1) * ck, vl])
               for j in range(n_new // ck)]
        if has_cache:
            out += [(lambda j=j: kc_ref[0, 0, j * ck:(j + 1) * ck, kl].astype(BF16),
                     lambda j=j: vc_ref[0, 0, j * ck:(j + 1) * ck, vl].astype(BF16))
                    for j in range(n_past // ck)]
        return out

    def place(t):
        if hpb == 1:
            return 0, pl.ds(pl.multiple_of(t * tq, tq), tq)
        return t // q_tiles, slice((t % q_tiles) * tq, (t % q_tiles + 1) * tq)

    rows = 2 * tq if diff else tq

    def cols(j):
        return slice(j * ck, (j + 1) * ck)


    def score_stage(t, slot, _, split):
        head, q_rows = place(t)
        q = q_ref[0, q_rows, head * dq:(head + 1) * dq]
        if diff:
            lane = lax.broadcasted_iota(jnp.int32, q.shape, 1)
            qf = q.astype(F32)
            q = jnp.concatenate([jnp.where(lane < DIFF_DH, qf, 0.0), jnp.where(lane >= DIFF_DH, qf, 0.0)],
                                axis=0).astype(BF16)
        m_acc = jnp.full((rows, LANES), -jnp.inf, F32)
        for j, (k_of, _) in enumerate(chunks_of(head)):
            s = lax.dot_general(q, k_of(), (((1,), (1,)), ((), ())), preferred_element_type=F32)
            s_bufs[slot][:, cols(j)] = s
            for u in range(ck // LANES):
                m_acc = jnp.maximum(m_acc, s[:, u * LANES:(u + 1) * LANES])
            yield
        return jnp.broadcast_to(jnp.max(m_acc, axis=-1, keepdims=True), (rows, LANES))

    def slab(j, u):
        return slice(j * ck + u * LANES, j * ck + (u + 1) * LANES)

    def exp_of(slot, j, m):
        return [jnp.exp2(s_bufs[slot][:, slab(j, u)] - m) for u in range(ck // LANES)]

    def softmax_value_stage(t, slot, m, split):
        head, q_rows = place(t)
        n_part = 2 if split else 1
        part = rows // n_part
        accs = [jnp.zeros((part, HEAD_W + LANES), F32)] * n_part
        ones = jnp.ones((ck, LANES), BF16)
        for j, (_, v_of) in enumerate(chunks_of(head)):
            e = jnp.concatenate([x.astype(BF16) for x in exp_of(slot, j, m)], axis=1)
            v1 = jnp.concatenate([v_of(), ones], axis=1)
            accs = [accs[i] + jnp.dot(e[i * part:(i + 1) * part], v1, preferred_element_type=F32)
                    for i in range(n_part)]
            yield
        acc = jnp.concatenate(accs, axis=0)
        o = acc[:, :HEAD_W] / acc[:, HEAD_W:]
        if diff:
            lmb = lmb_ref[...]
            lam = (jnp.exp(jnp.sum(lmb[0:1] * lmb[1:2], axis=-1, keepdims=True))
                   - jnp.exp(jnp.sum(lmb[2:3] * lmb[3:4], axis=-1, keepdims=True)) + lam_init)
            o = o[:tq] - lam * o[tq:]
            o = _rms(o, HEAD_W) * sub_ref[...] * (1.0 - lam_init)
        o_ref[0, q_rows, head * HEAD_W:(head + 1) * HEAD_W] = o.astype(BF16)

    stage_fns = (score_stage, softmax_value_stage)
    n_stage = len(stage_fns)
    n_slot = 2

    def step(u, u_static, inputs, split):
        gens = [(k, fn(u - k, (u_static - k) % n_slot, inputs[k], split))
                for k, fn in enumerate(stage_fns) if inputs[k] is not _IDLE]
        new = {}
        while gens:
            for item in list(gens):
                try:
                    next(item[1])
                except StopIteration as done:
                    new[item[0]] = done.value
                    gens.remove(item)
        return new

    def inputs_of(results, u):
        return [(results[k - 1] if k else None) if 0 <= u - k < n_tiles else _IDLE for k in range(n_stage)]

    n_steps = n_tiles + n_stage - 1
    steady = [u for u in range(n_steps) if _IDLE not in inputs_of([None] * n_stage, u)]
    n_iter = len(steady) // n_slot
    rolled = set(steady[:n_iter * n_slot]) if n_iter >= 2 else set()
    assert not rolled or hpb == 1

    results = [None] * n_stage
    u = 0
    while u < n_steps:
        if u in rolled:
            def body(i, res, u0=u):
                res = list(res) + [None]
                for r in range(n_slot):
                    new = step(u0 + i * n_slot + r, u0 + r, inputs_of(res, u0 + r), False)
                    res = [new[k] for k in range(n_stage)]
                return tuple(res[:-1])

            results = list(lax.fori_loop(0, n_iter, body, tuple(results[:-1]))) + [None]
            u += n_iter * n_slot
        else:
            new = step(u, u, inputs_of(results, u), bool(rolled) and u > max(rolled))
            results = [new.get(k) for k in range(n_stage)]
            u += 1


def _attention(q, k_new, v_new, cache, layer, *, diff, lam_init=0.0, lmb=None, subln=None):
    b_sz, n, _ = q.shape
    dq = LANES if diff else MLA_PAD
    n_heads = DIFF_HEADS if diff else MLA_HEADS
    tq = min(Q_TILE_DIFF if diff else Q_TILE_MLA, n)
    has_cache = cache is not None
    n_past = cache[0].shape[2] if has_cache else 0
    ck = min(KEY_CHUNK, n, n_past) if has_cache else min(KEY_CHUNK, n)
    assert n % tq == 0 and n % ck == 0 and n_past % ck == 0
    hpb = n_heads if n // tq == 1 else 1

    in_specs = [
        pl.BlockSpec((1, n, hpb * dq), lambda b, h: (b, 0, h)),
        pl.BlockSpec((1, n, hpb * dq), lambda b, h: (b, 0, h)),
        pl.BlockSpec((1, n, hpb * HEAD_W), lambda b, h: (b, 0, h)),
    ]
    args = [q, k_new, v_new]
    if has_cache:
        in_specs += [
            pl.BlockSpec((1, 1, n_past, hpb * dq), lambda b, h: (b, layer, 0, h)),
            pl.BlockSpec((1, 1, n_past, hpb * HEAD_W), lambda b, h: (b, layer, 0, h)),
        ]
        args += list(cache)
    if diff:
        in_specs += [_resident(lmb.shape), _resident(subln.shape)]
        args += [lmb, subln]
    rows = 2 * tq if diff else tq
    return pl.pallas_call(
        functools.partial(_attn_kernel, diff=diff, has_cache=has_cache, n_new=n, n_past=n_past,
                          tq=tq, ck=ck, hpb=hpb, lam_init=lam_init),
        grid=(b_sz, n_heads // hpb), in_specs=in_specs,
        out_specs=pl.BlockSpec((1, n, hpb * HEAD_W), lambda b, h: (b, 0, h)),
        out_shape=jax.ShapeDtypeStruct((b_sz, n, n_heads * HEAD_W), BF16),
        scratch_shapes=[pltpu.VMEM((rows, n + n_past), F32)] * 2,
        compiler_params=_params(2),
        name=("diff_attn" if diff else "mla_attn") + ("_latent" if has_cache else "_context"),
    )(*args)


def _log_sigmoid(x):
    return jnp.minimum(x, 0.0) - jnp.log(1.0 + jnp.exp(-jnp.abs(x)))


def _ret_kernel(*refs, latent, n_chunks, hpb):
    it = iter(refs)
    dec_ref, q_ref, k_ref, v_ref, g_ref, gn_ref = (next(it) for _ in range(6))
    if latent:
        r0_ref = next(it)
    o_ref = next(it)
    if not latent:
        st_ref = next(it)
    rcat_scr, u_scr = next(it), next(it)

    c_len = RET_CHUNK
    row = lax.broadcasted_iota(jnp.int32, (c_len, c_len), 0).astype(F32)
    col = lax.broadcasted_iota(jnp.int32, (c_len, c_len), 1).astype(F32)
    lane_fwd = lax.broadcasted_iota(jnp.int32, (c_len, 2 * RET_DK), 1) < RET_DK
    gn = gn_ref[...]
    unroll = min(RET_UNROLL, n_chunks)

    def chunk(c):
        return pl.ds(pl.multiple_of(c * c_len, c_len), c_len)

    def one_head(hh):
        hd = pl.program_id(1) * hpb + hh
        lanes = slice(hh * HEAD_W, (hh + 1) * HEAD_W)
        lg_f = _log_sigmoid(jnp.full((1, 1), dec_ref[0, hd], F32))
        lg_b = _log_sigmoid(jnp.full((1, 1), dec_ref[1, hd], F32))
        d_in = 0.5 * jnp.where(row >= col, jnp.exp((row - col) * lg_f), jnp.exp((col - row) * lg_b))
        q_dec = jnp.where(lane_fwd, jnp.exp((row + 1.0) * lg_f), jnp.exp((c_len - row) * lg_b))
        k_dec = jnp.where(lane_fwd, jnp.exp((c_len - 1.0 - row) * lg_f), jnp.exp(row * lg_b))
        dec_f = jnp.exp(c_len * lg_f)
        dec_b = jnp.exp(c_len * lg_b)

        def inc_body(c, carry):
            kd = (k_ref[0, chunk(c), lanes].astype(F32) * k_dec).astype(BF16)
            u_scr[hh, c] = lax.dot_general(kd, v_ref[0, chunk(c), lanes], (((0,), (0,)), ((), ())),
                                           preferred_element_type=F32)
            return carry

        lax.fori_loop(0, n_chunks, inc_body, 0, unroll=unroll)

        if latent:
            r_f0 = r0_ref[0, 0, 0, hh]
            r_b0 = r0_ref[0, 0, 1, hh]
        else:
            r_f0 = jnp.zeros((RET_DK, RET_DV), F32)
            r_b0 = r_f0

        def fwd_body(c, r):
            rcat_scr[hh, c, 0:RET_DK, :] = r.astype(BF16)
            return dec_f * r + u_scr[hh, c, 0:RET_DK, :]

        def bwd_body(t, r):
            c = n_chunks - 1 - t
            rcat_scr[hh, c, RET_DK:2 * RET_DK, :] = r.astype(BF16)
            return dec_b * r + u_scr[hh, c, RET_DK:2 * RET_DK, :]

        r_f = lax.fori_loop(0, n_chunks, fwd_body, r_f0, unroll=n_chunks <= 4)
        r_b = lax.fori_loop(0, n_chunks, bwd_body, r_b0, unroll=n_chunks <= 4)
        if not latent:
            st_ref[0, 0, hh] = r_f
            st_ref[0, 1, hh] = r_b

        def out_body(c, carry):
            qc = q_ref[0, chunk(c), lanes]
            kc = k_ref[0, chunk(c), lanes]
            vc = v_ref[0, chunk(c), lanes]
            a = lax.dot_general(qc, kc, (((1,), (1,)), ((), ())), preferred_element_type=F32) * d_in
            qd = (qc.astype(F32) * q_dec).astype(BF16)
            o = (jnp.dot(a.astype(BF16), vc, preferred_element_type=F32)
                 + jnp.dot(qd, rcat_scr[hh, c], preferred_element_type=F32))
            gate = g_ref[0, chunk(c), lanes].astype(F32)
            y = _rms(o, RET_DV) * gn * (gate * _sigmoid(gate))
            o_ref[0, chunk(c), lanes] = y.astype(BF16)
            return carry

        lax.fori_loop(0, n_chunks, out_body, 0, unroll=unroll)

    for hh in range(hpb):
        one_head(hh)


def _retention(rq, rk, rv, rg, ret_decay, ret_gn, state, layer):
    b_sz, n, _ = rq.shape
    latent = state is not None
    n_chunks = n // RET_CHUNK
    hpb = RET_HEADS if n_chunks <= 4 else 1
    tokw = pl.BlockSpec((1, n, hpb * HEAD_W), lambda b, h: (b, 0, h))
    in_specs = [pl.BlockSpec(memory_space=pltpu.SMEM), tokw, tokw, tokw, tokw, _resident(ret_gn.shape)]
    args = [ret_decay, rq, rk, rv, rg, ret_gn]
    if latent:
        in_specs.append(pl.BlockSpec((1, 1, 2, hpb, RET_DK, RET_DV), lambda b, h: (b, layer, 0, h, 0, 0)))
        args.append(state)
    out_specs = [tokw]
    out_shape = [jax.ShapeDtypeStruct((b_sz, n, RET_HEADS * HEAD_W), BF16)]
    if not latent:
        out_specs.append(pl.BlockSpec((1, 2, hpb, RET_DK, RET_DV), lambda b, h: (b, 0, h, 0, 0)))
        out_shape.append(jax.ShapeDtypeStruct((b_sz, 2, RET_HEADS, RET_DK, RET_DV), F32))
    return pl.pallas_call(
        functools.partial(_ret_kernel, latent=latent, n_chunks=n_chunks, hpb=hpb),
        grid=(b_sz, RET_HEADS // hpb), in_specs=in_specs, out_specs=out_specs, out_shape=out_shape,
        scratch_shapes=[pltpu.VMEM((hpb, n_chunks, RET_CHUNK, RET_DV), BF16),
                        pltpu.VMEM((hpb, n_chunks, RET_CHUNK, RET_DV), F32)],
        compiler_params=_params(2),
        name="retention_latent" if latent else "retention_context",
    )(*args)


def _ctxkeys_kernel(ckv_ref, kr_ref, w_kn, w_v, mkn_ref, mk_ref, mv_ref):
    ckvb = ckv_ref[0, 0].astype(BF16)
    kr = kr_ref[0, 0]
    k_nope = jnp.dot(ckvb, w_kn[...], preferred_element_type=F32)
    mv_ref[0, 0] = jnp.dot(ckvb, w_v[...], preferred_element_type=F32).astype(BF16)
    ss_kr = jnp.sum(kr * kr, axis=-1, keepdims=True)
    k_gain = mkn_ref[...]
    for hd in range(MLA_HEADS):
        kno = k_nope[:, hd * LANES:(hd + 1) * LANES]
        ss = jnp.sum(kno * kno, axis=-1, keepdims=True) + ss_kr
        rstd = lax.rsqrt(ss * (1.0 / MLA_QK) + EPS)
        mk_ref[0, 0, :, hd * MLA_PAD:hd * MLA_PAD + LANES] = (kno * rstd * k_gain[:, :LANES]).astype(BF16)
        mk_ref[0, 0, :, hd * MLA_PAD + LANES:(hd + 1) * MLA_PAD] = (kr * rstd * k_gain[:, LANES:]).astype(BF16)


def _context_keys(cache_ckv, cache_kr_pad, lws):
    b_sz, n_layers, p, _ = cache_ckv.shape
    outs = []
    for layer, lw in enumerate(lws):
        outs.append(pl.pallas_call(
            _ctxkeys_kernel,
            grid=(b_sz,),
            in_specs=[
                pl.BlockSpec((1, 1, p, MLA_KV_LORA), lambda b, layer=layer: (b, layer, 0, 0)),
                pl.BlockSpec((1, 1, p, LANES), lambda b, layer=layer: (b, layer, 0, 0)),
                _resident(lw["w_kn"].shape), _resident(lw["w_v"].shape), _resident(lw["mkn"].shape),
            ],
            out_specs=[pl.BlockSpec((1, 1, p, MLA_HEADS * MLA_PAD), lambda b: (b, 0, 0, 0)),
                       pl.BlockSpec((1, 1, p, MLA_HEADS * MLA_V), lambda b: (b, 0, 0, 0))],
            out_shape=[jax.ShapeDtypeStruct((b_sz, 1, p, MLA_HEADS * MLA_PAD), BF16),
                       jax.ShapeDtypeStruct((b_sz, 1, p, MLA_HEADS * MLA_V), BF16)],
            compiler_params=_params(1),
            name="mla_context_keys",
        )(cache_ckv, cache_kr_pad, lw["w_kn"], lw["w_v"], lw["mkn"]))
    return outs


def _post_kernel(x_ref, oa_ref, or_ref, om_ref, g_ref, mod_ref, n2_ref, wb_ref, wo_ref, wu_ref, wd_ref,
                 o_ref, *, ff_chunk):
    x = x_ref[0]
    mod = mod_ref[0, 0]
    d = x.shape[-1]
    merged = jnp.zeros(x.shape, F32)
    for i, br_ref in enumerate((oa_ref, or_ref, om_ref)):
        gate = _sigmoid(g_ref[0, :, i * d:(i + 1) * d].astype(F32))
        merged = merged + gate * jnp.dot(br_ref[0], wb_ref[i], preferred_element_type=F32)
    mix = jnp.dot(merged.astype(BF16), wo_ref[...], preferred_element_type=F32)
    x1 = x + mod[2:3] * mix
    h2 = (_rms(x1, d) * (n2_ref[...] * (1.0 + mod[4:5])) + mod[3:4]).astype(BF16)
    acc = jnp.zeros(x.shape, F32)
    for c in range(D_FF // ff_chunk):
        u = jnp.dot(h2, wu_ref[:, c * ff_chunk:(c + 1) * ff_chunk], preferred_element_type=F32)
        u = jnp.square(jnp.maximum(u, 0.0)).astype(BF16)
        acc = acc + jnp.dot(u, wd_ref[c * ff_chunk:(c + 1) * ff_chunk, :], preferred_element_type=F32)
    o_ref[0] = x1 + mod[5:6] * acc


def _post(x, oa, orr, om, gates, mod, layer, lw, latent):
    b_sz, n, d = x.shape
    tm = min(TOKEN_TILE, n)
    assert n % tm == 0

    def tok(width):
        return pl.BlockSpec((1, tm, width), lambda b, i: (b, i, 0))

    if latent:
        mod_spec = pl.BlockSpec((1, 1, 6, d), lambda b, i: (layer, 1 + b, 0, 0))
    else:
        mod_spec = pl.BlockSpec((1, 1, 6, d), lambda b, i: (layer, 0, 0, 0))
    weights = [lw["norm2"], lw["w_branch"], lw["w_out"], lw["w_up"], lw["w_down"]]
    return pl.pallas_call(
        functools.partial(_post_kernel, ff_chunk=1024),
        grid=(b_sz, n // tm),
        in_specs=[tok(d), tok(BRANCH_W), tok(BRANCH_W), tok(BRANCH_W), tok(N_BRANCH * d), mod_spec]
                 + [_resident(w.shape) for w in weights],
        out_specs=tok(d),
        out_shape=jax.ShapeDtypeStruct((b_sz, n, d), F32),
        compiler_params=_params(2),
        name="merge_mlp_latent" if latent else "merge_mlp_context",
    )(x, oa, orr, om, gates, mod, *weights)


def _rope_tables(n):
    pos = jnp.arange(n, dtype=jnp.int32)
    row = (pos // GRID_W).astype(F32)
    col = (pos % GRID_W).astype(F32)
    axis_dim = ROT_DIM // 2
    inv = ROPE_BASE ** (-jnp.arange(0, axis_dim, 2, dtype=F32) / axis_dim)
    ang_r = row[:, None] * inv[None, :]
    ang_c = col[:, None] * inv[None, :]
    zeros = jnp.zeros_like(ang_r)
    cos = jnp.concatenate([jnp.cos(ang_r)] * 2 + [jnp.cos(ang_c)] * 2, axis=-1)
    sin_a = jnp.concatenate([-jnp.sin(ang_r), zeros, -jnp.sin(ang_c), zeros], axis=-1)
    sin_b = jnp.concatenate([zeros, jnp.sin(ang_r), zeros, jnp.sin(ang_c)], axis=-1)
    reps = LANES // ROT_DIM
    return tuple(jnp.tile(t, (1, reps)) for t in (cos, sin_a, sin_b))


def _layer_weights(l, w_in, norm1, norm2, diff_qn, diff_kn, diff_lambda, diff_subln, ret_decay, ret_gn,
                   mla_qa_norm, w_mla_qb, mla_kva_norm, w_mla_kvb, mla_qn, mla_kn, w_branch, w_out, w_up,
                   w_down):
    d = w_in.shape[1]
    w = w_in[l]
    o_ret = 3 * BRANCH_W
    o_rk = o_ret + RET_HEADS * RET_DK
    o_rv = o_rk + RET_HEADS * RET_DK
    o_rg = o_rv + RET_HEADS * RET_DV
    o_qa = o_rg + RET_HEADS * RET_DV
    o_kva = o_qa + MLA_Q_LORA
    o_kr = o_kva + MLA_KV_LORA
    o_gate = o_kr + MLA_ROPE

    w_mla = jnp.concatenate([w[:, o_qa:o_gate], jnp.zeros((d, LANES - MLA_ROPE), w.dtype)], axis=-1)
    w_qb = jnp.pad(w_mla_qb[l].reshape(MLA_Q_LORA, MLA_HEADS, MLA_QK), ((0, 0), (0, 0), (0, MLA_PAD - MLA_QK)))
    w_kvb = w_mla_kvb[l].reshape(MLA_KV_LORA, MLA_HEADS, MLA_NOPE + MLA_V)

    def pad_gain(g):
        return jnp.pad(g, (0, MLA_PAD - MLA_QK))[None, :]

    return {
        "w_diff": w[:, :o_ret].astype(BF16),
        "w_ret": w[:, o_ret:o_qa].astype(BF16),
        "w_mla": w_mla.astype(BF16),
        "w_gate": w[:, o_gate:].astype(BF16),
        "w_qb": w_qb.reshape(MLA_Q_LORA, MLA_HEADS * MLA_PAD).astype(BF16),
        "w_kn": w_kvb[:, :, :MLA_NOPE].reshape(MLA_KV_LORA, MLA_HEADS * MLA_NOPE).astype(BF16),
        "w_v": w_kvb[:, :, MLA_NOPE:].reshape(MLA_KV_LORA, MLA_HEADS * MLA_V).astype(BF16),
        "norm1": norm1[l][None, :], "norm2": norm2[l][None, :],
        "dqn": jnp.tile(diff_qn[l], MXU_W // DIFF_DH)[None, :],
        "dkn": jnp.tile(diff_kn[l], MXU_W // DIFF_DH)[None, :],
        "qa_g": mla_qa_norm[l][None, :], "kva_g": mla_kva_norm[l][None, :],
        "mqn": pad_gain(mla_qn[l]), "mkn": pad_gain(mla_kn[l]),
        "lmb": diff_lambda[l], "subln": diff_subln[l][None, :],
        "ret_decay": ret_decay[l], "ret_gn": ret_gn[l][None, :],
        "w_branch": w_branch[l].astype(BF16), "w_out": w_out[l].astype(BF16),
        "w_up": w_up[l].astype(BF16), "w_down": w_down[l].astype(BF16),
    }


def _lambda_init(l):
    a, b, c = LAMBDA_INIT_BASE
    return a - b * math.exp(-c * l)


def _block(x, mod, layer, lw, rope_tabs, caches):
    latent = rope_tabs is not None
    b_sz, n, d = x.shape

    def flat(a):
        return a if latent else a.reshape(1, b_sz * n, a.shape[-1])

    outs = [o.reshape(b_sz, n, o.shape[-1]) for o in _inproj(flat(x), mod, layer, lw, rope_tabs)]
    dq, dk, dv, rq, rk, rv, rg, mq, mk, mv, gates = outs[:11]
    lam_init = _lambda_init(layer)
    diff_cache = caches["diff"] if latent else None
    mla_cache = caches["mla"][layer] if latent else None
    oa = _attention(dq, dk, dv, diff_cache, layer, diff=True, lam_init=lam_init, lmb=lw["lmb"],
                    subln=lw["subln"])
    om = _attention(mq, mk, mv, mla_cache, 0, diff=False)
    ret = _retention(rq, rk, rv, rg, lw["ret_decay"], lw["ret_gn"], caches["state"] if latent else None, layer)
    x = _post(flat(x), flat(oa), flat(ret[0]), flat(om), flat(gates), mod, layer, lw, latent).reshape(b_sz, n, d)
    if latent:
        return x, None
    return x, (outs[11], outs[12], outs[13], outs[14], ret[1])


def kernel(x_prompt, x_sample, cache_diff_k, cache_diff_v, cache_mla_ckv, cache_mla_krope, state_ret,
           c, c_ctx, w_mod, b_mod, norm1, norm2, w_in, diff_qn, diff_kn, diff_lambda, diff_subln,
           ret_decay, ret_gn, mla_qa_norm, w_mla_qb, mla_kva_norm, w_mla_kvb, mla_qn, mla_kn,
           w_branch, w_out, w_up, w_down):
    n_layers = w_in.shape[0]
    d = x_prompt.shape[-1]
    bd, n_lat = x_sample.shape[:2]
    b_ctx, n_ctx = x_prompt.shape[:2]
    p = cache_diff_k.shape[2]
    assert bd + 1 <= MOD_ROWS

    lws = [_layer_weights(l, w_in, norm1, norm2, diff_qn, diff_kn, diff_lambda, diff_subln, ret_decay, ret_gn,
                          mla_qa_norm, w_mla_qb, mla_kva_norm, w_mla_kvb, mla_qn, mla_kn, w_branch, w_out,
                          w_up, w_down) for l in range(n_layers)]

    cvec = jnp.concatenate([c_ctx[None, :], c, jnp.zeros((MOD_ROWS - 1 - bd, d), F32)], axis=0)
    mod = _modulation(cvec, w_mod, b_mod).reshape(n_layers, MOD_ROWS, 6, d)

    xp = x_prompt
    ctx_outs = []
    for l in range(n_layers):
        xp, ctx = _block(xp, mod, l, lws[l], None, None)
        ctx_outs.append(ctx)
    new_diff_k = jnp.stack([o[0] for o in ctx_outs], axis=1).reshape(b_ctx, n_layers, n_ctx, 2 * DIFF_HEADS, DIFF_DH)
    new_diff_v = jnp.stack([o[1] for o in ctx_outs], axis=1).reshape(b_ctx, n_layers, n_ctx, DIFF_HEADS, 2 * DIFF_DH)
    new_mla_ckv = jnp.stack([o[2] for o in ctx_outs], axis=1)
    new_mla_krope = jnp.stack([o[3] for o in ctx_outs], axis=1)
    new_state_ret = jnp.stack([o[4] for o in ctx_outs], axis=1)

    caches = {
        "diff": (cache_diff_k.reshape(bd, n_layers, p, 2 * DIFF_HEADS * DIFF_DH),
                 cache_diff_v.reshape(bd, n_layers, p, DIFF_HEADS * 2 * DIFF_DH)),
        "mla": _context_keys(cache_mla_ckv, jnp.pad(cache_mla_krope, ((0, 0),) * 3 + ((0, LANES - MLA_ROPE),)), lws),
        "state": state_ret,
    }
    rope_tabs = _rope_tables(n_lat)
    xs = x_sample
    for l in range(n_layers):
        xs, _ = _block(xs, mod, l, lws[l], rope_tabs, caches)

    return (xp, xs, new_diff_k, new_diff_v, new_mla_ckv, new_mla_krope, new_state_ret)
```

```python
import functools
import math

import jax
import jax.numpy as jnp
from jax import lax
from jax.experimental import pallas as pl
from jax.experimental.pallas import tpu as pltpu

F32 = jnp.float32
BF16 = jnp.bfloat16

D_MODEL = 1024
GRID_W = 64
DIFF_HEADS = 4
DIFF_DH = 64
RET_HEADS = 4
RET_DK = 64
RET_DV = 128
RET_CHUNK = 128
MLA_HEADS = 4
MLA_Q_LORA = 384
MLA_KV_LORA = 256
MLA_NOPE = 128
MLA_ROPE = 64
MLA_V = 128
MLA_QK = MLA_NOPE + MLA_ROPE
N_BRANCH = 3
BRANCH_W = 512
D_FF = 4 * D_MODEL
ROT_DIM = 64
ROPE_BASE = 10000.0
EPS = 1e-6
LAMBDA_INIT_BASE = (0.8, 0.6, 0.3)

LANES = 128
MXU_W = 256
MLA_PAD = 2 * LANES
HEAD_W = 128
LOG2E = 1.4426950408889634
VMEM_LIMIT = 56 * 1024 * 1024
TOKEN_TILE = 512
Q_TILE_DIFF = 512
Q_TILE_MLA = 1024
KEY_CHUNK = 512
RET_UNROLL = 32
_IDLE = object()
MOD_ROWS = 16


def _resident(shape):
    nd = len(shape)
    return pl.BlockSpec(shape, lambda *_: (0,) * nd, pipeline_mode=pl.Buffered(1))


def _params(n_grid):
    return pltpu.CompilerParams(dimension_semantics=("arbitrary",) * n_grid,
                                vmem_limit_bytes=VMEM_LIMIT)


def _sigmoid(x):
    return 1.0 / (1.0 + jnp.exp(-x))


def _rms(x, width):
    ss = jnp.sum(x * x, axis=-1, keepdims=True)
    return x * lax.rsqrt(ss * (1.0 / width) + EPS)


def _mod_kernel(c_ref, w_ref, b_ref, o_ref):
    c = c_ref[...]
    a = (c * _sigmoid(c)).astype(BF16)
    part = jnp.dot(a, w_ref[0].astype(BF16), preferred_element_type=F32)

    @pl.when(pl.program_id(1) == 0)
    def _():
        o_ref[0] = part + b_ref[0]

    @pl.when(pl.program_id(1) > 0)
    def _():
        o_ref[0] += part


def _modulation(cvec, w_mod, b_mod):
    n_layers, d, d6 = w_mod.shape
    tk = d // 4
    return pl.pallas_call(
        _mod_kernel,
        grid=(n_layers, d // tk),
        in_specs=[
            pl.BlockSpec((MOD_ROWS, tk), lambda l, k: (0, k)),
            pl.BlockSpec((1, tk, d6), lambda l, k: (l, k, 0)),
            pl.BlockSpec((1, 1, d6), lambda l, k: (l, 0, 0)),
        ],
        out_specs=pl.BlockSpec((1, MOD_ROWS, d6), lambda l, k: (l, 0, 0)),
        out_shape=jax.ShapeDtypeStruct((n_layers, MOD_ROWS, d6), F32),
        compiler_params=_params(2),
        name="modulation",
    )(cvec, w_mod, b_mod.reshape(n_layers, 1, d6))


def _inproj_kernel(*refs, latent):
    it = iter(refs)
    x_ref, mod_ref, n1_ref = next(it), next(it), next(it)
    if latent:
        cos_ref, sa_ref, sb_ref = next(it), next(it), next(it)
    w_diff, w_ret, w_mla, w_gate, w_qb, w_kn, w_v = (next(it) for _ in range(7))
    dqn_ref, dkn_ref, qag_ref, kvag_ref, mqn_ref, mkn_ref = (next(it) for _ in range(6))
    (dq_ref, dk_ref, dv_ref, rq_ref, rk_ref, rv_ref, rg_ref,
     mq_ref, mk_ref, mv_ref, gate_ref) = (next(it) for _ in range(11))
    if not latent:
        odk_ref, odv_ref, ockv_ref, okr_ref = (next(it) for _ in range(4))

    x = x_ref[0]
    mod = mod_ref[0, 0]
    h = _rms(x, D_MODEL) * (n1_ref[...] * (1.0 + mod[1:2])) + mod[0:1]
    hb = h.astype(BF16)

    if latent:
        cos, sa, sb = cos_ref[...], sa_ref[...], sb_ref[...]

        def rope(y):
            return (y * cos + pltpu.roll(y, LANES - 16, 1) * sa + pltpu.roll(y, 16, 1) * sb)
    else:
        def rope(y):
            return y

    def group(z, g):
        return z[:, g * LANES:(g + 1) * LANES]

    r_i = lax.broadcasted_iota(jnp.int32, (MXU_W, MXU_W), 0) // DIFF_DH
    c_i = lax.broadcasted_iota(jnp.int32, (MXU_W, MXU_W), 1) // DIFF_DH
    seg = jnp.where(r_i == c_i, 1.0, 0.0).astype(BF16)

    def qk_norm(y, gain):
        ss = jnp.dot((y * y).astype(BF16), seg, preferred_element_type=F32)
        return y * lax.rsqrt(ss * (1.0 / DIFF_DH) + EPS) * gain

    zd = jnp.dot(hb, w_diff[...], preferred_element_type=F32)
    n_g = BRANCH_W // LANES
    per_tile = MXU_W // LANES
    q_gain = dqn_ref[...] * (DIFF_DH ** -0.5 * LOG2E)
    for g2 in range(BRANCH_W // MXU_W):
        qn = qk_norm(zd[:, g2 * MXU_W:(g2 + 1) * MXU_W], q_gain)
        kn = qk_norm(zd[:, BRANCH_W + g2 * MXU_W:BRANCH_W + (g2 + 1) * MXU_W], dkn_ref[...])
        for g in range(g2 * per_tile, (g2 + 1) * per_tile):
            sl = slice(g * LANES, (g + 1) * LANES)
            dq_ref[0, :, sl] = rope(group(qn, g % per_tile)).astype(BF16)
            dk_ref[0, :, sl] = rope(group(kn, g % per_tile)).astype(BF16)
            v = group(zd, 2 * n_g + g)
            dv_ref[0, :, sl] = v.astype(BF16)
            if not latent:
                odk_ref[0, :, sl] = group(kn, g % per_tile)
                odv_ref[0, :, sl] = v

    zr = jnp.dot(hb, w_ret[...], preferred_element_type=F32)
    low = lax.broadcasted_iota(jnp.int32, (x.shape[0], LANES), 1) < RET_DK

    def twice(y):
        swapped = pltpu.roll(y, RET_DK, 1)
        return jnp.where(low, y, swapped), jnp.where(low, swapped, y)

    n_p = RET_HEADS * RET_DK // LANES
    for p in range(n_p):
        q_pair = twice(rope(group(zr, p)))
        k_pair = twice(rope(group(zr, n_p + p)) * RET_DK ** -0.5)
        for i in range(2):
            sl = slice((2 * p + i) * LANES, (2 * p + i + 1) * LANES)
            rq_ref[0, :, sl] = q_pair[i].astype(BF16)
            rk_ref[0, :, sl] = k_pair[i].astype(BF16)
    for g in range(n_g):
        sl = slice(g * LANES, (g + 1) * LANES)
        rv_ref[0, :, sl] = group(zr, 2 * n_p + g).astype(BF16)
        rg_ref[0, :, sl] = group(zr, 2 * n_p + n_g + g).astype(BF16)

    zm = jnp.dot(hb, w_mla[...], preferred_element_type=F32)
    qa = _rms(zm[:, :MLA_Q_LORA], MLA_Q_LORA) * qag_ref[...]
    mqz = jnp.dot(qa.astype(BF16), w_qb[...], preferred_element_type=F32)
    ckv = _rms(zm[:, MLA_Q_LORA:MLA_Q_LORA + MLA_KV_LORA], MLA_KV_LORA) * kvag_ref[...]
    ckvb = ckv.astype(BF16)
    k_nope = jnp.dot(ckvb, w_kn[...], preferred_element_type=F32)
    mv_ref[0] = jnp.dot(ckvb, w_v[...], preferred_element_type=F32).astype(BF16)
    kr = zm[:, MLA_Q_LORA + MLA_KV_LORA:]
    ss_kr = jnp.sum(kr * kr, axis=-1, keepdims=True)
    if not latent:
        ockv_ref[0] = ckv
        okr_ref[0] = kr[:, :MLA_ROPE]
    q_gain = mqn_ref[...] * (MLA_QK ** -0.5 * LOG2E)
    k_gain = mkn_ref[...]
    for hd in range(MLA_HEADS):
        lo = slice(hd * MLA_PAD, hd * MLA_PAD + LANES)
        hi = slice(hd * MLA_PAD + LANES, (hd + 1) * MLA_PAD)
        qh = _rms(mqz[:, hd * MLA_PAD:(hd + 1) * MLA_PAD], MLA_QK) * q_gain
        mq_ref[0, :, lo] = qh[:, :LANES].astype(BF16)
        mq_ref[0, :, hi] = rope(qh[:, LANES:]).astype(BF16)
        kno = k_nope[:, hd * LANES:(hd + 1) * LANES]
        ss = jnp.sum(kno * kno, axis=-1, keepdims=True) + ss_kr
        rstd = lax.rsqrt(ss * (1.0 / MLA_QK) + EPS)
        mk_ref[0, :, lo] = (kno * rstd * k_gain[:, :LANES]).astype(BF16)
        mk_ref[0, :, hi] = rope(kr * rstd * k_gain[:, LANES:]).astype(BF16)

    gate_ref[0] = jnp.dot(hb, w_gate[...], preferred_element_type=F32).astype(BF16)


def _inproj(x, mod, layer, lw, rope_tabs):
    b_sz, n, d = x.shape
    latent = rope_tabs is not None
    tm = min(TOKEN_TILE, n)
    assert n % tm == 0
    grid = (b_sz, n // tm)

    def tok(width):
        return pl.BlockSpec((1, tm, width), lambda b, i: (b, i, 0))

    if latent:
        mod_spec = pl.BlockSpec((1, 1, 6, d), lambda b, i: (layer, 1 + b, 0, 0))
    else:
        mod_spec = pl.BlockSpec((1, 1, 6, d), lambda b, i: (layer, 0, 0, 0))
    in_specs = [tok(d), mod_spec, _resident((1, d))]
    args = [x, mod, lw["norm1"]]
    if latent:
        in_specs += [pl.BlockSpec((tm, LANES), lambda b, i: (i, 0))] * 3
        args += list(rope_tabs)
    weights = [lw["w_diff"], lw["w_ret"], lw["w_mla"], lw["w_gate"], lw["w_qb"], lw["w_kn"], lw["w_v"]]
    gains = [lw["dqn"], lw["dkn"], lw["qa_g"], lw["kva_g"], lw["mqn"], lw["mkn"]]
    for a in weights + gains:
        in_specs.append(_resident(a.shape))
        args.append(a)

    widths = [BRANCH_W] * 7 + [MLA_HEADS * MLA_PAD] * 2 + [MLA_HEADS * MLA_V, N_BRANCH * d]
    out_specs = [tok(w) for w in widths]
    out_shape = [jax.ShapeDtypeStruct((b_sz, n, w), BF16) for w in widths]
    if not latent:
        for w in (BRANCH_W, BRANCH_W, MLA_KV_LORA, MLA_ROPE):
            out_specs.append(tok(w))
            out_shape.append(jax.ShapeDtypeStruct((b_sz, n, w), F32))
    return pl.pallas_call(
        functools.partial(_inproj_kernel, latent=latent),
        grid=grid, in_specs=in_specs, out_specs=out_specs, out_shape=out_shape,
        compiler_params=_params(2),
        name="inproj_latent" if latent else "inproj_context",
    )(*args)


def _attn_kernel(*refs, diff, has_cache, n_new, n_past, tq, ck, hpb, lam_init):
    it = iter(refs)
    q_ref, kn_ref, vn_ref = next(it), next(it), next(it)
    if has_cache:
        kc_ref, vc_ref = next(it), next(it)
    if diff:
        lmb_ref, sub_ref = next(it), next(it)
    o_ref = next(it)
    s_bufs = (next(it), next(it))
    q_tiles = n_new // tq
    n_tiles = hpb * q_tiles
    dq = q_ref.shape[-1] // hpb

    def chunks_of(head):
        kl = slice(head * dq, (head + 1) * dq)
        vl = slice(head * HEAD_W, (head + 1) * HEAD_W)
        out = [(lambda j=j: kn_ref[0, j * ck:(j + 1) * ck, kl], lambda j=j: vn_ref[0, j * ck:(j + 1) * ck, vl])
               for j in range(n_new // ck)]
        if has_cache:
            out += [(lambda j=j: kc_ref[0, 0, j * ck:(j + 1) * ck, kl].astype(BF16),
                     lambda j=j: vc_ref[0, 0, j * ck:(j + 1) * ck, vl].astype(BF16))
                    for j in range(n_past // ck)]
        return out

    def place(t):
        if hpb == 1:
            return 0, pl.ds(pl.multiple_of(t * tq, tq), tq)
        return t // q_tiles, slice((t % q_tiles) * tq, (t % q_tiles + 1) * tq)

    rows = 2 * tq if diff else tq

    def cols(j):
        return slice(j * ck, (j + 1) * ck)


    def score_stage(t, slot, _, split):
        head, q_rows = place(t)
        q = q_ref[0, q_rows, head * dq:(head + 1) * dq]
        if diff:
            lane = lax.broadcasted_iota(jnp.int32, q.shape, 1)
            qf = q.astype(F32)
            q = jnp.concatenate([jnp.where(lane < DIFF_DH, qf, 0.0), jnp.where(lane >= DIFF_DH, qf, 0.0)],
                                axis=0).astype(BF16)
        m_acc = jnp.full((rows, LANES), -jnp.inf, F32)
        for j, (k_of, _) in enumerate(chunks_of(head)):
            s = lax.dot_general(q, k_of(), (((1,), (1,)), ((), ())), preferred_element_type=F32)
            s_bufs[slot][:, cols(j)] = s
            for u in range(ck // LANES):
                m_acc = jnp.maximum(m_acc, s[:, u * LANES:(u + 1) * LANES])
            yield
        return jnp.broadcast_to(jnp.max(m_acc, axis=-1, keepdims=True), (rows, LANES))

    def slab(j, u):
        return slice(j * ck + u * LANES, j * ck + (u + 1) * LANES)

    def exp_of(slot, j, m):
        return [jnp.exp2(s_bufs[slot][:, slab(j, u)] - m) for u in range(ck // LANES)]

    def softmax_value_stage(t, slot, m, split):
        head, q_rows = place(t)
        n_part = 2 if split else 1
        part = rows // n_part
        accs = [jnp.zeros((part, HEAD_W + LANES), F32)] * n_part
        ones = jnp.ones((ck, LANES), BF16)
        for j, (_, v_of) in enumerate(chunks_of(head)):
            e = jnp.concatenate([x.astype(BF16) for x in exp_of(slot, j, m)], axis=1)
            v1 = jnp.concatenate([v_of(), ones], axis=1)
            accs = [accs[i] + jnp.dot(e[i * part:(i + 1) * part], v1, preferred_element_type=F32)
                    for i in range(n_part)]
            yield
        acc = jnp.concatenate(accs, axis=0)
        o = acc[:, :HEAD_W] / acc[:, HEAD_W:]
        if diff:
            lmb = lmb_ref[...]
            lam = (jnp.exp(jnp.sum(lmb[0:1] * lmb[1:2], axis=-1, keepdims=True))
                   - jnp.exp(jnp.sum(lmb[2:3] * lmb[3:4], axis=-1, keepdims=True)) + lam_init)
            o = o[:tq] - lam * o[tq:]
            o = _rms(o, HEAD_W) * sub_ref[...] * (1.0 - lam_init)
        o_ref[0, q_rows, head * HEAD_W:(head + 1) * HEAD_W] = o.astype(BF16)

    stage_fns = (score_stage, softmax_value_stage)
    n_stage = len(stage_fns)
    n_slot = 2

    def step(u, u_static, inputs, split):
        gens = [(k, fn(u - k, (u_static - k) % n_slot, inputs[k], split))
                for k, fn in enumerate(stage_fns) if inputs[k] is not _IDLE]
        new = {}
        while gens:
            for item in list(gens):
                try:
                    next(item[1])
                except StopIteration as done:
                    new[item[0]] = done.value
                    gens.remove(item)
        return new

    def inputs_of(results, u):
        return [(results[k - 1] if k else None) if 0 <= u - k < n_tiles else _IDLE for k in range(n_stage)]

    n_steps = n_tiles + n_stage - 1
    steady = [u for u in range(n_steps) if _IDLE not in inputs_of([None] * n_stage, u)]
    n_iter = len(steady) // n_slot
    rolled = set(steady[:n_iter * n_slot]) if n_iter >= 2 else set()
    assert not rolled or hpb == 1

    results = [None] * n_stage
    u = 0
    while u < n_steps:
        if u in rolled:
            def body(i, res, u0=u):
                res = list(res) + [None]
                for r in range(n_slot):
                    new = step(u0 + i * n_slot + r, u0 + r, inputs_of(res, u0 + r), False)
                    res = [new[k] for k in range(n_stage)]
                return tuple(res[:-1])

            results = list(lax.fori_loop(0, n_iter, body, tuple(results[:-1]))) + [None]
            u += n_iter * n_slot
        else:
            new = step(u, u, inputs_of(results, u), bool(rolled) and u > max(rolled))
            results = [new.get(k) for k in range(n_stage)]
            u += 1


def _attention(q, k_new, v_new, cache, layer, *, diff, lam_init=0.0, lmb=None, subln=None):
    b_sz, n, _ = q.shape
    dq = LANES if diff else MLA_PAD
    n_heads = DIFF_HEADS if diff else MLA_HEADS
    tq = min(Q_TILE_DIFF if diff else Q_TILE_MLA, n)
    has_cache = cache is not None
    n_past = cache[0].shape[2] if has_cache else 0
    ck = min(KEY_CHUNK, n, n_past) if has_cache else min(KEY_CHUNK, n)
    assert n % tq == 0 and n % ck == 0 and n_past % ck == 0
    hpb = n_heads if n // tq == 1 else 1

    in_specs = [
        pl.BlockSpec((1, n, hpb * dq), lambda b, h: (b, 0, h)),
        pl.BlockSpec((1, n, hpb * dq), lambda b, h: (b, 0, h)),
        pl.BlockSpec((1, n, hpb * HEAD_W), lambda b, h: (b, 0, h)),
    ]
    args = [q, k_new, v_new]
    if has_cache:
        in_specs += [
            pl.BlockSpec((1, 1, n_past, hpb * dq), lambda b, h: (b, layer, 0, h)),
            pl.BlockSpec((1, 1, n_past, hpb * HEAD_W), lambda b, h: (b, layer, 0, h)),
        ]
        args += list(cache)
    if diff:
        in_specs += [_resident(lmb.shape), _resident(subln.shape)]
        args += [lmb, subln]
    rows = 2 * tq if diff else tq
    return pl.pallas_call(
        functools.partial(_attn_kernel, diff=diff, has_cache=has_cache, n_new=n, n_past=n_past,
                          tq=tq, ck=ck, hpb=hpb, lam_init=lam_init),
        grid=(b_sz, n_heads // hpb), in_specs=in_specs,
        out_specs=pl.BlockSpec((1, n, hpb * HEAD_W), lambda b, h: (b, 0, h)),
        out_shape=jax.ShapeDtypeStruct((b_sz, n, n_heads * HEAD_W), BF16),
        scratch_shapes=[pltpu.VMEM((rows, n + n_past), F32)] * 2,
        compiler_params=_params(2),
        name=("diff_attn" if diff else "mla_attn") + ("_latent" if has_cache else "_context"),
    )(*args)


def _log_sigmoid(x):
    return jnp.minimum(x, 0.0) - jnp.log(1.0 + jnp.exp(-jnp.abs(x)))


def _ret_kernel(*refs, latent, n_chunks, hpb):
    it = iter(refs)
    dec_ref, q_ref, k_ref, v_ref, g_ref, gn_ref = (next(it) for _ in range(6))
    if latent:
        r0_ref = next(it)
    o_ref = next(it)
    if not latent:
        st_ref = next(it)
    rcat_scr, u_scr = next(it), next(it)

    c_len = RET_CHUNK
    row = lax.broadcasted_iota(jnp.int32, (c_len, c_len), 0).astype(F32)
    col = lax.broadcasted_iota(jnp.int32, (c_len, c_len), 1).astype(F32)
    lane_fwd = lax.broadcasted_iota(jnp.int32, (c_len, 2 * RET_DK), 1) < RET_DK
    gn = gn_ref[...]
    unroll = min(RET_UNROLL, n_chunks)

    def chunk(c):
        return pl.ds(pl.multiple_of(c * c_len, c_len), c_len)

    def one_head(hh):
        hd = pl.program_id(1) * hpb + hh
        lanes = slice(hh * HEAD_W, (hh + 1) * HEAD_W)
        lg_f = _log_sigmoid(jnp.full((1, 1), dec_ref[0, hd], F32))
        lg_b = _log_sigmoid(jnp.full((1, 1), dec_ref[1, hd], F32))
        d_in = 0.5 * jnp.where(row >= col, jnp.exp((row - col) * lg_f), jnp.exp((col - row) * lg_b))
        q_dec = jnp.where(lane_fwd, jnp.exp((row + 1.0) * lg_f), jnp.exp((c_len - row) * lg_b))
        k_dec = jnp.where(lane_fwd, jnp.exp((c_len - 1.0 - row) * lg_f), jnp.exp(row * lg_b))
        dec_f = jnp.exp(c_len * lg_f)
        dec_b = jnp.exp(c_len * lg_b)

        def inc_body(c, carry):
            kd = (k_ref[0, chunk(c), lanes].astype(F32) * k_dec).astype(BF16)
            u_scr[hh, c] = lax.dot_general(kd, v_ref[0, chunk(c), lanes], (((0,), (0,)), ((), ())),
                                           preferred_element_type=F32)
            return carry

        lax.fori_loop(0, n_chunks, inc_body, 0, unroll=unroll)

        if latent:
            r_f0 = r0_ref[0, 0, 0, hh]
            r_b0 = r0_ref[0, 0, 1, hh]
        else:
            r_f0 = jnp.zeros((RET_DK, RET_DV), F32)
            r_b0 = r_f0

        def fwd_body(c, r):
            rcat_scr[hh, c, 0:RET_DK, :] = r.astype(BF16)
            return dec_f * r + u_scr[hh, c, 0:RET_DK, :]

        def bwd_body(t, r):
            c = n_chunks - 1 - t
            rcat_scr[hh, c, RET_DK:2 * RET_DK, :] = r.astype(BF16)
            return dec_b * r + u_scr[hh, c, RET_DK:2 * RET_DK, :]

        r_f = lax.fori_loop(0, n_chunks, fwd_body, r_f0, unroll=n_chunks <= 4)
        r_b = lax.fori_loop(0, n_chunks, bwd_body, r_b0, unroll=n_chunks <= 4)
        if not latent:
            st_ref[0, 0, hh] = r_f
            st_ref[0, 1, hh] = r_b

        def out_body(c, carry):
            qc = q_ref[0, chunk(c), lanes]
            kc = k_ref[0, chunk(c), lanes]
            vc = v_ref[0, chunk(c), lanes]
            a = lax.dot_general(qc, kc, (((1,), (1,)), ((), ())), preferred_element_type=F32) * d_in
            qd = (qc.astype(F32) * q_dec).astype(BF16)
            o = (jnp.dot(a.astype(BF16), vc, preferred_element_type=F32)
                 + jnp.dot(qd, rcat_scr[hh, c], preferred_element_type=F32))
            gate = g_ref[0, chunk(c), lanes].astype(F32)
            y = _rms(o, RET_DV) * gn * (gate * _sigmoid(gate))
            o_ref[0, chunk(c), lanes] = y.astype(BF16)
            return carry

        lax.fori_loop(0, n_chunks, out_body, 0, unroll=unroll)

    for hh in range(hpb):
        one_head(hh)


def _retention(rq, rk, rv, rg, ret_decay, ret_gn, state, layer):
    b_sz, n, _ = rq.shape
    latent = state is not None
    n_chunks = n // RET_CHUNK
    hpb = RET_HEADS if n_chunks <= 4 else 1
    tokw = pl.BlockSpec((1, n, hpb * HEAD_W), lambda b, h: (b, 0, h))
    in_specs = [pl.BlockSpec(memory_space=pltpu.SMEM), tokw, tokw, tokw, tokw, _resident(ret_gn.shape)]
    args = [ret_decay, rq, rk, rv, rg, ret_gn]
    if latent:
        in_specs.append(pl.BlockSpec((1, 1, 2, hpb, RET_DK, RET_DV), lambda b, h: (b, layer, 0, h, 0, 0)))
        args.append(state)
    out_specs = [tokw]
    out_shape = [jax.ShapeDtypeStruct((b_sz, n, RET_HEADS * HEAD_W), BF16)]
    if not latent:
        out_specs.append(pl.BlockSpec((1, 2, hpb, RET_DK, RET_DV), lambda b, h: (b, 0, h, 0, 0)))
        out_shape.append(jax.ShapeDtypeStruct((b_sz, 2, RET_HEADS, RET_DK, RET_DV), F32))
    return pl.pallas_call(
        functools.partial(_ret_kernel, latent=latent, n_chunks=n_chunks, hpb=hpb),
        grid=(b_sz, RET_HEADS // hpb), in_specs=in_specs, out_specs=out_specs, out_shape=out_shape,
        scratch_shapes=[pltpu.VMEM((hpb, n_chunks, RET_CHUNK, RET_DV), BF16),
                        pltpu.VMEM((hpb, n_chunks, RET_CHUNK, RET_DV), F32)],
        compiler_params=_params(2),
        name="retention_latent" if latent else "retention_context",
    )(*args)


def _ctxkeys_kernel(ckv_ref, kr_ref, w_kn, w_v, mkn_ref, mk_ref, mv_ref):
    ckvb = ckv_ref[0, 0].astype(BF16)
    kr = kr_ref[0, 0]
    k_nope = jnp.dot(ckvb, w_kn[...], preferred_element_type=F32)
    mv_ref[0, 0] = jnp.dot(ckvb, w_v[...], preferred_element_type=F32).astype(BF16)
    ss_kr = jnp.sum(kr * kr, axis=-1, keepdims=True)
    k_gain = mkn_ref[...]
    for hd in range(MLA_HEADS):
        kno = k_nope[:, hd * LANES:(hd + 1) * LANES]
        ss = jnp.sum(kno * kno, axis=-1, keepdims=True) + ss_kr
        rstd = lax.rsqrt(ss * (1.0 / MLA_QK) + EPS)
        mk_ref[0, 0, :, hd * MLA_PAD:hd * MLA_PAD + LANES] = (kno * rstd * k_gain[:, :LANES]).astype(BF16)
        mk_ref[0, 0, :, hd * MLA_PAD + LANES:(hd + 1) * MLA_PAD] = (kr * rstd * k_gain[:, LANES:]).astype(BF16)


def _context_keys(cache_ckv, cache_kr_pad, lws):
    b_sz, n_layers, p, _ = cache_ckv.shape
    outs = []
    for layer, lw in enumerate(lws):
        outs.append(pl.pallas_call(
            _ctxkeys_kernel,
            grid=(b_sz,),
            in_specs=[
                pl.BlockSpec((1, 1, p, MLA_KV_LORA), lambda b, layer=layer: (b, layer, 0, 0)),
                pl.BlockSpec((1, 1, p, LANES), lambda b, layer=layer: (b, layer, 0, 0)),
                _resident(lw["w_kn"].shape), _resident(lw["w_v"].shape), _resident(lw["mkn"].shape),
            ],
            out_specs=[pl.BlockSpec((1, 1, p, MLA_HEADS * MLA_PAD), lambda b: (b, 0, 0, 0)),
                       pl.BlockSpec((1, 1, p, MLA_HEADS * MLA_V), lambda b: (b, 0, 0, 0))],
            out_shape=[jax.ShapeDtypeStruct((b_sz, 1, p, MLA_HEADS * MLA_PAD), BF16),
                       jax.ShapeDtypeStruct((b_sz, 1, p, MLA_HEADS * MLA_V), BF16)],
            compiler_params=_params(1),
            name="mla_context_keys",
        )(cache_ckv, cache_kr_pad, lw["w_kn"], lw["w_v"], lw["mkn"]))
    return outs


def _post_kernel(x_ref, oa_ref, or_ref, om_ref, g_ref, mod_ref, n2_ref, wb_ref, wo_ref, wu_ref, wd_ref,
                 o_ref, *, ff_chunk):
    x = x_ref[0]
    mod = mod_ref[0, 0]
    d = x.shape[-1]
    merged = jnp.zeros(x.shape, F32)
    for i, br_ref in enumerate((oa_ref, or_ref, om_ref)):
        gate = _sigmoid(g_ref[0, :, i * d:(i + 1) * d].astype(F32))
        merged = merged + gate * jnp.dot(br_ref[0], wb_ref[i], preferred_element_type=F32)
    mix = jnp.dot(merged.astype(BF16), wo_ref[...], preferred_element_type=F32)
    x1 = x + mod[2:3] * mix
    h2 = (_rms(x1, d) * (n2_ref[...] * (1.0 + mod[4:5])) + mod[3:4]).astype(BF16)
    acc = jnp.zeros(x.shape, F32)
    for c in range(D_FF // ff_chunk):
        u = jnp.dot(h2, wu_ref[:, c * ff_chunk:(c + 1) * ff_chunk], preferred_element_type=F32)
        u = jnp.square(jnp.maximum(u, 0.0)).astype(BF16)
        acc = acc + jnp.dot(u, wd_ref[c * ff_chunk:(c + 1) * ff_chunk, :], preferred_element_type=F32)
    o_ref[0] = x1 + mod[5:6] * acc


def _post(x, oa, orr, om, gates, mod, layer, lw, latent):
    b_sz, n, d = x.shape
    tm = min(TOKEN_TILE, n)
    assert n % tm == 0

    def tok(width):
        return pl.BlockSpec((1, tm, width), lambda b, i: (b, i, 0))

    if latent:
        mod_spec = pl.BlockSpec((1, 1, 6, d), lambda b, i: (layer, 1 + b, 0, 0))
    else:
        mod_spec = pl.BlockSpec((1, 1, 6, d), lambda b, i: (layer, 0, 0, 0))
    weights = [lw["norm2"], lw["w_branch"], lw["w_out"], lw["w_up"], lw["w_down"]]
    return pl.pallas_call(
        functools.partial(_post_kernel, ff_chunk=1024),
        grid=(b_sz, n // tm),
        in_specs=[tok(d), tok(BRANCH_W), tok(BRANCH_W), tok(BRANCH_W), tok(N_BRANCH * d), mod_spec]
                 + [_resident(w.shape) for w in weights],
        out_specs=tok(d),
        out_shape=jax.ShapeDtypeStruct((b_sz, n, d), F32),
        compiler_params=_params(2),
        name="merge_mlp_latent" if latent else "merge_mlp_context",
    )(x, oa, orr, om, gates, mod, *weights)


def _rope_tables(n):
    pos = jnp.arange(n, dtype=jnp.int32)
    row = (pos // GRID_W).astype(F32)
    col = (pos % GRID_W).astype(F32)
    axis_dim = ROT_DIM // 2
    inv = ROPE_BASE ** (-jnp.arange(0, axis_dim, 2, dtype=F32) / axis_dim)
    ang_r = row[:, None] * inv[None, :]
    ang_c = col[:, None] * inv[None, :]
    zeros = jnp.zeros_like(ang_r)
    cos = jnp.concatenate([jnp.cos(ang_r)] * 2 + [jnp.cos(ang_c)] * 2, axis=-1)
    sin_a = jnp.concatenate([-jnp.sin(ang_r), zeros, -jnp.sin(ang_c), zeros], axis=-1)
    sin_b = jnp.concatenate([zeros, jnp.sin(ang_r), zeros, jnp.sin(ang_c)], axis=-1)
    reps = LANES // ROT_DIM
    return tuple(jnp.tile(t, (1, reps)) for t in (cos, sin_a, sin_b))


def _layer_weights(l, w_in, norm1, norm2, diff_qn, diff_kn, diff_lambda, diff_subln, ret_decay, ret_gn,
                   mla_qa_norm, w_mla_qb, mla_kva_norm, w_mla_kvb, mla_qn, mla_kn, w_branch, w_out, w_up,
                   w_down):
    d = w_in.shape[1]
    w = w_in[l]
    o_ret = 3 * BRANCH_W
    o_rk = o_ret + RET_HEADS * RET_DK
    o_rv = o_rk + RET_HEADS * RET_DK
    o_rg = o_rv + RET_HEADS * RET_DV
    o_qa = o_rg + RET_HEADS * RET_DV
    o_kva = o_qa + MLA_Q_LORA
    o_kr = o_kva + MLA_KV_LORA
    o_gate = o_kr + MLA_ROPE

    w_mla = jnp.concatenate([w[:, o_qa:o_gate], jnp.zeros((d, LANES - MLA_ROPE), w.dtype)], axis=-1)
    w_qb = jnp.pad(w_mla_qb[l].reshape(MLA_Q_LORA, MLA_HEADS, MLA_QK), ((0, 0), (0, 0), (0, MLA_PAD - MLA_QK)))
    w_kvb = w_mla_kvb[l].reshape(MLA_KV_LORA, MLA_HEADS, MLA_NOPE + MLA_V)

    def pad_gain(g):
        return jnp.pad(g, (0, MLA_PAD - MLA_QK))[None, :]

    return {
        "w_diff": w[:, :o_ret].astype(BF16),
        "w_ret": w[:, o_ret:o_qa].astype(BF16),
        "w_mla": w_mla.astype(BF16),
        "w_gate": w[:, o_gate:].astype(BF16),
        "w_qb": w_qb.reshape(MLA_Q_LORA, MLA_HEADS * MLA_PAD).astype(BF16),
        "w_kn": w_kvb[:, :, :MLA_NOPE].reshape(MLA_KV_LORA, MLA_HEADS * MLA_NOPE).astype(BF16),
        "w_v": w_kvb[:, :, MLA_NOPE:].reshape(MLA_KV_LORA, MLA_HEADS * MLA_V).astype(BF16),
        "norm1": norm1[l][None, :], "norm2": norm2[l][None, :],
        "dqn": jnp.tile(diff_qn[l], MXU_W // DIFF_DH)[None, :],
        "dkn": jnp.tile(diff_kn[l], MXU_W // DIFF_DH)[None, :],
        "qa_g": mla_qa_norm[l][None, :], "kva_g": mla_kva_norm[l][None, :],
        "mqn": pad_gain(mla_qn[l]), "mkn": pad_gain(mla_kn[l]),
        "lmb": diff_lambda[l], "subln": diff_subln[l][None, :],
        "ret_decay": ret_decay[l], "ret_gn": ret_gn[l][None, :],
        "w_branch": w_branch[l].astype(BF16), "w_out": w_out[l].astype(BF16),
        "w_up": w_up[l].astype(BF16), "w_down": w_down[l].astype(BF16),
    }


def _lambda_init(l):
    a, b, c = LAMBDA_INIT_BASE
    return a - b * math.exp(-c * l)


def _block(x, mod, layer, lw, rope_tabs, caches):
    latent = rope_tabs is not None
    b_sz, n, d = x.shape

    def flat(a):
        return a if latent else a.reshape(1, b_sz * n, a.shape[-1])

    outs = [o.reshape(b_sz, n, o.shape[-1]) for o in _inproj(flat(x), mod, layer, lw, rope_tabs)]
    dq, dk, dv, rq, rk, rv, rg, mq, mk, mv, gates = outs[:11]
    lam_init = _lambda_init(layer)
    diff_cache = caches["diff"] if latent else None
    mla_cache = caches["mla"][layer] if latent else None
    oa = _attention(dq, dk, dv, diff_cache, layer, diff=True, lam_init=lam_init, lmb=lw["lmb"],
                    subln=lw["subln"])
    om = _attention(mq, mk, mv, mla_cache, 0, diff=False)
    ret = _retention(rq, rk, rv, rg, lw["ret_decay"], lw["ret_gn"], caches["state"] if latent else None, layer)
    x = _post(flat(x), flat(oa), flat(ret[0]), flat(om), flat(gates), mod, layer, lw, latent).reshape(b_sz, n, d)
    if latent:
        return x, None
    return x, (outs[11], outs[12], outs[13], outs[14], ret[1])


def kernel(x_prompt, x_sample, cache_diff_k, cache_diff_v, cache_mla_ckv, cache_mla_krope, state_ret,
           c, c_ctx, w_mod, b_mod, norm1, norm2, w_in, diff_qn, diff_kn, diff_lambda, diff_subln,
           ret_decay, ret_gn, mla_qa_norm, w_mla_qb, mla_kva_norm, w_mla_kvb, mla_qn, mla_kn,
           w_branch, w_out, w_up, w_down):
    n_layers = w_in.shape[0]
    d = x_prompt.shape[-1]
    bd, n_lat = x_sample.shape[:2]
    b_ctx, n_ctx = x_prompt.shape[:2]
    p = cache_diff_k.shape[2]
    assert bd + 1 <= MOD_ROWS

    lws = [_layer_weights(l, w_in, norm1, norm2, diff_qn, diff_kn, diff_lambda, diff_subln, ret_decay, ret_gn,
                          mla_qa_norm, w_mla_qb, mla_kva_norm, w_mla_kvb, mla_qn, mla_kn, w_branch, w_out,
                          w_up, w_down) for l in range(n_layers)]

    cvec = jnp.concatenate([c_ctx[None, :], c, jnp.zeros((MOD_ROWS - 1 - bd, d), F32)], axis=0)
    mod = _modulation(cvec, w_mod, b_mod).reshape(n_layers, MOD_ROWS, 6, d)

    xp = x_prompt
    ctx_outs = []
    for l in range(n_layers):
        xp, ctx = _block(xp, mod, l, lws[l], None, None)
        ctx_outs.append(ctx)
    new_diff_k = jnp.stack([o[0] for o in ctx_outs], axis=1).reshape(b_ctx, n_layers, n_ctx, 2 * DIFF_HEADS, DIFF_DH)
    new_diff_v = jnp.stack([o[1] for o in ctx_outs], axis=1).reshape(b_ctx, n_layers, n_ctx, DIFF_HEADS, 2 * DIFF_DH)
    new_mla_ckv = jnp.stack([o[2] for o in ctx_outs], axis=1)
    new_mla_krope = jnp.stack([o[3] for o in ctx_outs], axis=1)
    new_state_ret = jnp.stack([o[4] for o in ctx_outs], axis=1)

    caches = {
        "diff": (cache_diff_k.reshape(bd, n_layers, p, 2 * DIFF_HEADS * DIFF_DH),
                 cache_diff_v.reshape(bd, n_layers, p, DIFF_HEADS * 2 * DIFF_DH)),
        "mla": _context_keys(cache_mla_ckv, jnp.pad(cache_mla_krope, ((0, 0),) * 3 + ((0, LANES - MLA_ROPE),)), lws),
        "state": state_ret,
    }
    rope_tabs = _rope_tables(n_lat)
    xs = x_sample
    for l in range(n_layers):
        xs, _ = _block(xs, mod, l, lws[l], rope_tabs, caches)

    return (xp, xs, new_diff_k, new_diff_v, new_mla_ckv, new_mla_krope, new_state_ret)
```
